```python
import jax
import jax.numpy as jnp
from jax import lax
import numpy as np

D_MODEL = 1024
BATCH = 32
SEQ = 2048
DEPTH = 4

MEM_LEN = 256
NORM_EPS = 1e-6
D_FF = 2816
A_HEADS = 8
A_HEAD_DIM = 64
A_WIDTH = A_HEADS * A_HEAD_DIM
A_RANK_W = 64
A_RANK_A = 64
A_RANK_G = 128
A_LN_EPS = 64e-5
B_HEADS = 8
B_QK_DIM = 64
B_V_DIM = 128
B_QK_WIDTH = B_HEADS * B_QK_DIM
B_V_WIDTH = B_HEADS * B_V_DIM
B_CHUNK = 128
B_GN_EPS = 1e-5
ROPE_BASE = 10000.0
X_HEADS = 4
X_HEAD_DIM = D_MODEL // X_HEADS
A_SHIFT_SIZES = (A_WIDTH, A_WIDTH, A_WIDTH, A_RANK_W, A_RANK_A, A_RANK_G)
A_SHIFT_WIDTH = sum(A_SHIFT_SIZES)
B_SIZES = (B_QK_WIDTH, B_QK_WIDTH, B_V_WIDTH, B_V_WIDTH)
B_IN_WIDTH = sum(B_SIZES)
IN_WIDTH = A_SHIFT_WIDTH + B_IN_WIDTH + 2 * D_MODEL

kernel_name = "hybrid_rwkv7_retention_macaron"


def _split(z, sizes):
    return jnp.split(z, [int(s) for s in np.cumsum(sizes)[:-1]], axis=-1)


def rms_norm(x, g):
    xf = x.astype(jnp.float32)
    y = xf * lax.rsqrt(jnp.mean(xf * xf, axis=-1, keepdims=True) + NORM_EPS)
    return (y * g.astype(jnp.float32)).astype(x.dtype)


def _standardize(x, eps):
    mu = jnp.mean(x, axis=-1, keepdims=True)
    xc = x - mu
    return xc * lax.rsqrt(jnp.mean(xc * xc, axis=-1, keepdims=True) + eps)


def swiglu_ffn(x, w_in, w_out):
    gate, up = jnp.split(x @ w_in, 2, axis=-1)
    return (jax.nn.silu(gate) * up) @ w_out


def token_shift(z):
    return jnp.pad(z[:, :-1], ((0, 0), (1, 0), (0, 0)))


def rotary(x, positions):
    half = x.shape[-1] // 2
    inv_freq = ROPE_BASE ** (-jnp.arange(half, dtype=jnp.float32) / half)
    ang = positions.astype(jnp.float32)[:, :, None] * inv_freq
    cos = jnp.cos(ang)[:, :, None, :]
    sin = jnp.sin(ang)[:, :, None, :]
    x1, x2 = x[..., :half], x[..., half:]
    return jnp.concatenate([x1 * cos - x2 * sin, x1 * sin + x2 * cos], axis=-1)


def rwkv7_scan(r, decay, k, v, kk, a):
    B, T, H, N = r.shape

    def step(S, inp):
        r_t, w_t, k_t, v_t, kk_t, a_t = inp
        sa = jnp.einsum('bhij,bhj->bhi', S, -kk_t)
        S = (S * w_t[:, :, None, :]
             + sa[..., None] * (kk_t * a_t)[:, :, None, :]
             + v_t[..., None] * k_t[:, :, None, :])
        return S, jnp.einsum('bhij,bhj->bhi', S, r_t)

    xs = tuple(jnp.moveaxis(t, 1, 0) for t in (r, decay, k, v, kk, a))
    _, y = lax.scan(step, jnp.zeros((B, H, N, N), jnp.float32), xs)
    return jnp.moveaxis(y, 0, 1)


def rwkv7_time_mix(z_a, w0, w_up, a0, a_up, g_up, k_k, k_a, r_k, ln):
    B, T, _ = z_a.shape
    dtype = z_a.dtype
    f32 = jnp.float32
    r, k, v, w_lo, a_lo, g_lo = _split(z_a.astype(f32), A_SHIFT_SIZES)
    w = w0.astype(f32) + jnp.tanh(w_lo) @ w_up.astype(f32)
    decay = jnp.exp(-jnp.exp(-jax.nn.softplus(-w) - 0.5))
    a = jax.nn.sigmoid(a0.astype(f32) + a_lo @ a_up.astype(f32))
    g = jax.nn.sigmoid(g_lo) @ g_up.astype(f32)

    def heads(t):
        return t.reshape(B, T, A_HEADS, A_HEAD_DIM)

    kk = heads(k * k_k.astype(f32))
    kk = kk / jnp.maximum(jnp.sqrt(jnp.sum(kk * kk, axis=-1, keepdims=True)), 1e-12)
    k = k * (1.0 + (a - 1.0) * k_a.astype(f32))
    rh, kh, vh = heads(r), heads(k), heads(v)
    y = rwkv7_scan(rh, heads(decay), kh, vh, kk, heads(a))
    ln = ln.astype(f32)
    y = _standardize(y, A_LN_EPS).reshape(B, T, A_WIDTH) * ln[0] + ln[1]
    bonus = jnp.sum(rh * kh * r_k.astype(f32), axis=-1, keepdims=True) * vh
    y = y + bonus.reshape(B, T, A_WIDTH)
    return (y * g).astype(dtype)


def retention_chunkwise(q, k, v):
    B, T, H, dk = q.shape
    dv = v.shape[-1]
    C = B_CHUNK
    nc = T // C
    log_gamma = jnp.log1p(-jnp.exp2(-5.0 - jnp.arange(H, dtype=jnp.float32)))
    qc = q.reshape(B, nc, C, H, dk)
    kc = k.reshape(B, nc, C, H, dk)
    vc = v.reshape(B, nc, C, H, dv)
    pos = jnp.arange(C, dtype=jnp.float32)
    diff = pos[:, None] - pos[None, :]
    inner_decay = jnp.where(diff[None] >= 0,
                            jnp.exp(jnp.maximum(diff, 0.0)[None] * log_gamma[:, None, None]),
                            0.0)
    scores = jnp.einsum('bnchd,bnshd->bnhcs', qc, kc) * inner_decay[None, None]
    inner = jnp.einsum('bnhcs,bnshe->bnche', scores, vc)
    q_decay = jnp.exp((pos + 1.0)[None, :] * log_gamma[:, None])
    k_decay = jnp.exp((C - 1.0 - pos)[None, :] * log_gamma[:, None])
    chunk_decay = jnp.exp(C * log_gamma)
    kv = jnp.einsum('bnshd,bnshe->bnhde', kc * k_decay.T[None, None, :, :, None], vc)

    def step(R, kv_n):
        return R * chunk_decay[None, :, None, None] + kv_n, R

    _, r_prev = lax.scan(step, jnp.zeros((B, H, dk, dv), jnp.float32), jnp.moveaxis(kv, 1, 0))
    r_prev = jnp.moveaxis(r_prev, 0, 1)
    cross = jnp.einsum('bnchd,bnhde->bnche', qc * q_decay.T[None, None, :, :, None], r_prev)
    return (inner + cross).reshape(B, T, H, dv)


def retention_mix(z_b, positions, gn):
    B, T, _ = z_b.shape
    dtype = z_b.dtype
    q, k, v, g = _split(z_b.astype(jnp.float32), B_SIZES)
    qh = rotary(q.reshape(B, T, B_HEADS, B_QK_DIM), positions)
    kh = rotary(k.reshape(B, T, B_HEADS, B_QK_DIM), positions) * (B_QK_DIM ** -0.5)
    y = retention_chunkwise(qh, kh, v.reshape(B, T, B_HEADS, B_V_DIM))
    y = _standardize(y, B_GN_EPS).reshape(B, T, B_V_WIDTH) * gn.astype(jnp.float32)
    return (y * jax.nn.silu(g)).astype(dtype)


def memory_cross_attn(hn, mem, mem_g, w_q, w_kv, w_o):
    B, T, _ = hn.shape
    q = (hn @ w_q).reshape(B, T, X_HEADS, X_HEAD_DIM)
    k, v = jnp.split(rms_norm(mem, mem_g) @ w_kv, 2, axis=-1)
    k = k.reshape(B, MEM_LEN, X_HEADS, X_HEAD_DIM)
    v = v.reshape(B, MEM_LEN, X_HEADS, X_HEAD_DIM)
    s = jnp.einsum('bthd,bmhd->bhtm', q, k).astype(jnp.float32) * (X_HEAD_DIM ** -0.5)
    p = jax.nn.softmax(s, axis=-1).astype(v.dtype)
    o = jnp.einsum('bhtm,bmhd->bthd', p, v).reshape(B, T, D_MODEL)
    return o @ w_o


def setup_inputs(seed: int = 0) -> dict:
    key = jax.random.key(seed)
    ks = jax.random.split(key, 32)
    f32 = jnp.float32

    def nrm(k, shape, scale):
        return jax.random.normal(k, shape, f32) * scale

    x = nrm(ks[0], (BATCH, SEQ, D_MODEL), 1.0)
    mem = nrm(ks[1], (BATCH, MEM_LEN, D_MODEL), 1.0)
    offset = jax.random.randint(ks[2], (BATCH, 1), 0, 4096, dtype=jnp.int32)
    positions = offset + jnp.arange(SEQ, dtype=jnp.int32)[None, :]
    return {
        "x": x,
        "mem": mem,
        "positions": positions,
        "norm_g": 1.0 + nrm(ks[3], (DEPTH, 4, D_MODEL), 0.02),
        "ffn_w_in": nrm(ks[4], (DEPTH, 2, D_MODEL, 2 * D_FF), D_MODEL ** -0.5),
        "ffn_w_out": nrm(ks[5], (DEPTH, 2, D_FF, D_MODEL), 0.5 * D_FF ** -0.5),
        "mix_w_in": nrm(ks[6], (DEPTH, D_MODEL, IN_WIDTH), D_MODEL ** -0.5),
        "mix_gate_b": nrm(ks[7], (DEPTH, 2, D_MODEL), 0.02),
        "shift_mu": jax.random.uniform(ks[8], (DEPTH, A_SHIFT_WIDTH), f32),
        "a_w0": jax.random.uniform(ks[9], (DEPTH, A_WIDTH), f32, -6.0, -1.0),
        "a_w_up": nrm(ks[10], (DEPTH, A_RANK_W, A_WIDTH), 0.1 * A_RANK_W ** -0.5),
        "a_a0": nrm(ks[11], (DEPTH, A_WIDTH), 0.1),
        "a_a_up": nrm(ks[12], (DEPTH, A_RANK_A, A_WIDTH), 0.1 * A_RANK_A ** -0.5),
        "a_g_up": nrm(ks[13], (DEPTH, A_RANK_G, A_WIDTH), A_RANK_G ** -0.5),
        "a_k_k": 0.85 + nrm(ks[14], (DEPTH, A_WIDTH), 0.05),
        "a_k_a": 1.0 + nrm(ks[15], (DEPTH, A_WIDTH), 0.05),
        "a_r_k": nrm(ks[16], (DEPTH, A_HEADS, A_HEAD_DIM), 0.1),
        "a_ln": jnp.stack([1.0 + nrm(ks[17], (DEPTH, A_WIDTH), 0.02),
                           nrm(ks[18], (DEPTH, A_WIDTH), 0.02)], axis=1),
        "b_gn": 1.0 + nrm(ks[19], (DEPTH, B_V_WIDTH), 0.02),
        "w_branch_a": nrm(ks[20], (DEPTH, A_WIDTH, D_MODEL), A_WIDTH ** -0.5),
        "w_branch_b": nrm(ks[21], (DEPTH, B_V_WIDTH, D_MODEL), B_V_WIDTH ** -0.5),
        "mix_w_out": nrm(ks[22], (DEPTH, D_MODEL, D_MODEL), 0.5 * D_MODEL ** -0.5),
        "mem_norm": 1.0 + nrm(ks[23], (DEPTH, D_MODEL), 0.02),
        "cross_w_q": nrm(ks[24], (DEPTH, D_MODEL, D_MODEL), D_MODEL ** -0.5),
        "cross_w_kv": nrm(ks[25], (DEPTH, D_MODEL, 2 * D_MODEL), D_MODEL ** -0.5),
        "cross_w_o": nrm(ks[26], (DEPTH, D_MODEL, D_MODEL), 0.5 * D_MODEL ** -0.5),
        "final_norm": 1.0 + nrm(ks[27], (D_MODEL,), 0.02),
    }


def reference(x, mem, positions, norm_g, ffn_w_in, ffn_w_out, mix_w_in, mix_gate_b, shift_mu,
              a_w0, a_w_up, a_a0, a_a_up, a_g_up, a_k_k, a_k_a, a_r_k, a_ln, b_gn,
              w_branch_a, w_branch_b, mix_w_out, mem_norm, cross_w_q, cross_w_kv, cross_w_o,
              final_norm):
    h = x
    for l in range(DEPTH):
        h = h + 0.5 * swiglu_ffn(rms_norm(h, norm_g[l, 0]), ffn_w_in[l, 0], ffn_w_out[l, 0])
        u = rms_norm(h, norm_g[l, 1])
        z = u @ mix_w_in[l]
        z_a = z[..., :A_SHIFT_WIDTH]
        z_a = z_a + (token_shift(z_a) - z_a) * shift_mu[l]
        z_b = z[..., A_SHIFT_WIDTH:A_SHIFT_WIDTH + B_IN_WIDTH]
        g_pre = z[..., A_SHIFT_WIDTH + B_IN_WIDTH:]
        y_a = rwkv7_time_mix(z_a, a_w0[l], a_w_up[l], a_a0[l], a_a_up[l], a_g_up[l],
                             a_k_k[l], a_k_a[l], a_r_k[l], a_ln[l])
        y_b = retention_mix(z_b, positions, b_gn[l])
        gates = jax.nn.sigmoid(g_pre.astype(jnp.float32)
                               + mix_gate_b[l].reshape(-1).astype(jnp.float32)).astype(h.dtype)
        gate_a, gate_b = jnp.split(gates, 2, axis=-1)
        merged = gate_a * (y_a @ w_branch_a[l]) + gate_b * (y_b @ w_branch_b[l])
        h = h + merged @ mix_w_out[l]
        h = h + memory_cross_attn(rms_norm(h, norm_g[l, 2]), mem, mem_norm[l],
                                  cross_w_q[l], cross_w_kv[l], cross_w_o[l])
        h = h + 0.5 * swiglu_ffn(rms_norm(h, norm_g[l, 3]), ffn_w_in[l, 1], ffn_w_out[l, 1])
    return rms_norm(h, final_norm)
```

```python
import functools
import math

import jax
import jax.numpy as jnp
import numpy as np
from jax import lax
from jax.experimental import pallas as pl
from jax.experimental.pallas import tpu as pltpu

F32 = jnp.float32
BF16 = jnp.bfloat16

NORM_EPS = 1e-6
D_FF = 2816
A_HEADS = 8
A_HEAD_DIM = 64
A_WIDTH = A_HEADS * A_HEAD_DIM
A_RANK_W = 64
A_RANK_A = 64
A_RANK_G = 128
A_LN_EPS = 64e-5
A_SHIFT_WIDTH = 3 * A_WIDTH + A_RANK_W + A_RANK_A + A_RANK_G
B_HEADS = 8
B_QK_DIM = 64
B_V_DIM = 128
B_QK_WIDTH = B_HEADS * B_QK_DIM
B_V_WIDTH = B_HEADS * B_V_DIM
B_IN_WIDTH = 2 * B_QK_WIDTH + 2 * B_V_WIDTH
B_CHUNK = 128
B_GN_EPS = 1e-5
ROPE_BASE = 10000.0
X_HEADS = 4

RWKV_CHUNK = 64
RWKV_SOLVE_LEVELS = 6
LANES = 128
VMEM_LIMIT = 56 * 1024 * 1024

FFN_TM = 512
FFN_TF = 256
INPROJ_TM = 256
SEQ_TB = 256
MERGE_TM = 512
CROSS_TM = 512
ROPE_TM = 1024


def _const_spec(shape):
    nd = len(shape)
    return pl.BlockSpec(shape, lambda *_: (0,) * nd, pipeline_mode=pl.Buffered(1))


def _params(*sem):
    return pltpu.CompilerParams(dimension_semantics=sem, vmem_limit_bytes=VMEM_LIMIT)


def _pieces(x, n):
    if x.dtype == BF16:
        return [x]
    out, rem = [], x
    for i in range(n):
        p = rem.astype(BF16)
        out.append(p)
        if i + 1 < n:
            rem = rem - p.astype(F32)
    return out


_NN = (((1,), (0,)), ((), ()))
_NT = (((1,), (1,)), ((), ()))
_TN = (((0,), (0,)), ((), ()))


def _mm(a, b, dims=_NN, pa=1, pb=1):
    acc = None
    for x in _pieces(a, pa):
        for y in _pieces(b, pb):
            t = lax.dot_general(x, y, dims, preferred_element_type=F32)
            acc = t if acc is None else acc + t
    return acc


def _rms(x, g):
    ms = jnp.mean(x * x, axis=-1, keepdims=True)
    return x * lax.rsqrt(ms + NORM_EPS) * g


def _sigmoid(x):
    return 1.0 / (1.0 + jnp.exp(-x))


def _ffn_kernel(h_ref, g_ref, wg_ref, wu_ref, wo_ref, fg_ref, o_ref, act_ref, *, final):
    h = h_ref[...]
    u = _rms(h, g_ref[...]).astype(BF16)
    for c in range(D_FF // FFN_TF):
        sl = pl.ds(c * FFN_TF, FFN_TF)
        gate = jnp.dot(u, wg_ref[:, sl], preferred_element_type=F32)
        up = jnp.dot(u, wu_ref[:, sl], preferred_element_type=F32)
        act_ref[:, sl] = (gate * _sigmoid(gate) * up).astype(BF16)
    y = jnp.dot(act_ref[...], wo_ref[...], preferred_element_type=F32)
    out = h + 0.5 * y
    if final:
        out = _rms(out, fg_ref[...])
    o_ref[...] = out


def _ffn(h, g, w_in, w_out, final_g, final):
    n, d = h.shape
    return pl.pallas_call(
        functools.partial(_ffn_kernel, final=final),
        grid=(n // FFN_TM,),
        in_specs=[
            pl.BlockSpec((FFN_TM, d), lambda i: (i, 0)),
            _const_spec((1, d)),
            pl.BlockSpec((d, D_FF), lambda i: (0, 0), pipeline_mode=pl.Buffered(1)),
            pl.BlockSpec((d, D_FF), lambda i: (0, 1), pipeline_mode=pl.Buffered(1)),
            _const_spec((D_FF, d)),
            _const_spec((1, d)),
        ],
        out_specs=pl.BlockSpec((FFN_TM, d), lambda i: (i, 0)),
        out_shape=jax.ShapeDtypeStruct((n, d), F32),
        scratch_shapes=[pltpu.VMEM((FFN_TM, D_FF), BF16)],
        compiler_params=_params("parallel"),
        name="ffn",
    )(h, g.reshape(1, d), w_in, w_in, w_out, final_g.reshape(1, d))


def _inproj_kernel(h_ref, g_ref, w_ref, za_ref, zb_ref, gp_ref):
    u = _rms(h_ref[...], g_ref[...]).astype(BF16)
    c0, c1 = A_SHIFT_WIDTH, A_SHIFT_WIDTH + B_IN_WIDTH
    za_ref[...] = jnp.dot(u, w_ref[:, :c0], preferred_element_type=F32)
    zb_ref[...] = jnp.dot(u, w_ref[:, c0:c1], preferred_element_type=F32)
    gp_ref[...] = jnp.dot(u, w_ref[:, c1:], preferred_element_type=F32)


def _inproj(h, g, w):
    n, d = h.shape
    widths = (A_SHIFT_WIDTH, B_IN_WIDTH, w.shape[1] - A_SHIFT_WIDTH - B_IN_WIDTH)
    return pl.pallas_call(
        _inproj_kernel,
        grid=(n // INPROJ_TM,),
        in_specs=[
            pl.BlockSpec((INPROJ_TM, d), lambda i: (i, 0)),
            _const_spec((1, d)),
            _const_spec(w.shape),
        ],
        out_specs=[pl.BlockSpec((INPROJ_TM, wd), lambda i: (i, 0)) for wd in widths],
        out_shape=[jax.ShapeDtypeStruct((n, wd), F32) for wd in widths],
        compiler_params=_params("parallel"),
        name="inproj",
    )(h, g.reshape(1, d), w)


def _seg_sum(x, seg_ref):
    return _mm(x, seg_ref[...], pa=2)


def _rwkv_kernel(za_ref, mu_ref, w0_ref, wup_ref, a0_ref, aup_ref, gup_ref, kk_ref, ka_ref, rk_ref, ln_ref,
                 seg_ref, tri_ref, y_ref, carry_ref, state_ref):
    L = RWKV_CHUNK
    hd = A_HEAD_DIM

    @pl.when(pl.program_id(1) == 0)
    def _():
        carry_ref[...] = jnp.zeros_like(carry_ref)
        state_ref[...] = jnp.zeros_like(state_ref)

    z = za_ref[...]
    tb = z.shape[0]
    row = lax.broadcasted_iota(jnp.int32, (tb, 1), 0)
    prev = jnp.where(row == 0, carry_ref[...], pltpu.roll(z, 1, 0))
    carry_ref[...] = z[tb - 1:tb, :]
    xs = z + (prev - z) * mu_ref[...]

    r = xs[:, 0:A_WIDTH]
    k = xs[:, A_WIDTH:2 * A_WIDTH]
    v = xs[:, 2 * A_WIDTH:3 * A_WIDTH]
    o = 3 * A_WIDTH
    w_lo = xs[:, o:o + A_RANK_W]
    a_lo = xs[:, o + A_RANK_W:o + A_RANK_W + A_RANK_A]
    g_lo = xs[:, o + A_RANK_W + A_RANK_A:]

    w = w0_ref[...] + _mm(jnp.tanh(w_lo), wup_ref[...])
    logd = -_sigmoid(w) * math.exp(-0.5)
    a = _sigmoid(a0_ref[...] + _mm(a_lo, aup_ref[...]))
    g = _mm(_sigmoid(g_lo), gup_ref[...])

    kk = k * kk_ref[...]
    kk = kk / jnp.maximum(jnp.sqrt(_seg_sum(kk * kk, seg_ref)), 1e-12)
    k = k * (1.0 + (a - 1.0) * ka_ref[...])
    kka = kk * a

    rows2 = lax.broadcasted_iota(jnp.int32, (L, 2 * L), 0)
    cols2 = lax.broadcasted_iota(jnp.int32, (L, 2 * L), 1) & (L - 1)
    strict_lower = cols2 < rows2
    lower = cols2 <= rows2
    eye = (lax.broadcasted_iota(jnp.int32, (hd, hd), 0) == lax.broadcasted_iota(jnp.int32, (hd, hd), 1))
    zeros_lh = jnp.zeros((L, hd), F32)

    for c in range(tb // L):
        rs = slice(c * L, (c + 1) * L)
        ld = logd[rs]
        cum = _mm(tri_ref[...], ld, pb=3)
        c_end = cum[L - 1:L, :]
        e_inc = jnp.exp(cum)
        e_exc = jnp.exp(cum - ld)
        e_neg = jnp.exp(-cum)
        e_end = jnp.exp(c_end - cum)
        p_end = jnp.exp(c_end)
        rt = r[rs] * e_inc
        at = -kk[rs] * e_exc
        bt = kka[rs] * e_neg
        kt = k[rs] * e_neg
        bh = kka[rs] * e_end
        kh = k[rs] * e_end
        vc = v[rs]
        outs = []
        for h in range(A_HEADS):
            hs = slice(h * hd, (h + 1) * hd)
            at_h, rt_h, v_h = at[:, hs], rt[:, hs], vc[:, hs]
            m1 = _mm(jnp.concatenate([at_h, rt_h], axis=0), jnp.concatenate([bt[:, hs], kt[:, hs]], axis=0), _NT)
            a_a = jnp.where(strict_lower, m1[:L], 0.0)
            a_r = jnp.where(lower, m1[L:], 0.0)
            pw = a_a[:, :L]
            sol = jnp.concatenate([at_h, _mm(a_a[:, L:], v_h)], axis=1)
            for lvl in range(RWKV_SOLVE_LEVELS):
                if lvl + 1 < RWKV_SOLVE_LEVELS:
                    prod = _mm(pw, jnp.concatenate([sol, pw], axis=1))
                    sol = sol + prod[:, :2 * hd]
                    pw = prod[:, 2 * hd:]
                else:
                    sol = sol + _mm(pw, sol)
            rhs2 = jnp.concatenate([sol, jnp.concatenate([zeros_lh, v_h], axis=1)], axis=0)
            m2a = _mm(a_r, rhs2)
            m2b = _mm(jnp.concatenate([bh[:, hs], kh[:, hs]], axis=0), rhs2, _TN)
            qp = rt_h + m2a[:, :hd]
            gmat = jnp.where(eye, p_end[:, hs], 0.0) + m2b[:, :hd]
            m3 = _mm(jnp.concatenate([qp, gmat], axis=0), state_ref[h])
            outs.append(m3[:L] + m2a[:, hd:])
            state_ref[h] = m3[L:] + m2b[:, hd:]
        y_ref[rs, :] = jnp.concatenate(outs, axis=1)

    y = y_ref[...]
    inv_n = 1.0 / hd
    mean = _seg_sum(y, seg_ref) * inv_n
    yc = y - mean
    var = _seg_sum(yc * yc, seg_ref) * inv_n
    yn = yc * lax.rsqrt(var + A_LN_EPS) * ln_ref[0:1, :] + ln_ref[1:2, :]
    bonus = _seg_sum(r * k * rk_ref[...], seg_ref) * v
    y_ref[...] = (yn + bonus) * g


def _rwkv(za, batch, mu, w0, w_up, a0, a_up, g_up, k_k, k_a, r_k, ln, seg, tri):
    n = za.shape[0]
    nt = n // batch // SEQ_TB
    row = lambda p: p.reshape(1, -1)
    consts = [row(mu), row(w0), w_up, row(a0), a_up, g_up, row(k_k), row(k_a), row(r_k), ln, seg, tri]
    return pl.pallas_call(
        _rwkv_kernel,
        grid=(batch, nt),
        in_specs=[pl.BlockSpec((SEQ_TB, A_SHIFT_WIDTH), lambda b, t: (b * nt + t, 0))]
        + [_const_spec(c.shape) for c in consts],
        out_specs=pl.BlockSpec((SEQ_TB, A_WIDTH), lambda b, t: (b * nt + t, 0)),
        out_shape=jax.ShapeDtypeStruct((n, A_WIDTH), F32),
        scratch_shapes=[pltpu.VMEM((1, A_SHIFT_WIDTH), F32), pltpu.VMEM((A_HEADS, A_HEAD_DIM, A_HEAD_DIM), F32)],
        compiler_params=_params("parallel", "arbitrary"),
        name="rwkv",
    )(za, *consts)


def _rope_kernel(pos_ref, invf_ref, sign_ref, cos_ref, sin_ref):
    ang = pos_ref[...].astype(F32) * invf_ref[...]
    cos_ref[...] = jnp.cos(ang)
    sin_ref[...] = jnp.sin(ang) * sign_ref[...]


def _rope_tables(pos, invf, sign):
    n = pos.shape[0]
    return pl.pallas_call(
        _rope_kernel,
        grid=(n // ROPE_TM,),
        in_specs=[pl.BlockSpec((ROPE_TM, 1), lambda i: (i, 0)), _const_spec((1, LANES)), _const_spec((1, LANES))],
        out_specs=[pl.BlockSpec((ROPE_TM, LANES), lambda i: (i, 0))] * 2,
        out_shape=[jax.ShapeDtypeStruct((n, LANES), F32)] * 2,
        compiler_params=_params("parallel"),
        name="rope",
    )(pos, invf, sign)


def _ret_kernel(zb_ref, cos_ref, sin_ref, dmask_ref, qdec_ref, kdec_ref, gn_ref, y_ref, state_ref, *, chunk_decay):
    C = B_CHUNK
    dk, dv = B_QK_DIM, B_V_DIM

    @pl.when(pl.program_id(1) == 0)
    def _():
        state_ref[...] = jnp.zeros_like(state_ref)

    first_half = (lax.broadcasted_iota(jnp.int32, (C, B_QK_WIDTH), 1) & (dk - 1)) < dk // 2

    def rotate(x, cos, sin):
        partner = jnp.where(first_half, pltpu.roll(x, B_QK_WIDTH - dk // 2, 1), pltpu.roll(x, dk // 2, 1))
        return x * cos + partner * sin

    for c in range(zb_ref.shape[0] // C):
        rs = slice(c * C, (c + 1) * C)
        reps = B_QK_WIDTH // LANES
        cos = jnp.concatenate([cos_ref[rs, :]] * reps, axis=1)
        sin = jnp.concatenate([sin_ref[rs, :]] * reps, axis=1)
        q = rotate(zb_ref[rs, 0:B_QK_WIDTH], cos, sin)
        k = rotate(zb_ref[rs, B_QK_WIDTH:2 * B_QK_WIDTH], cos, sin) * (dk ** -0.5)
        qd = q * qdec_ref[...]
        kd = k * kdec_ref[...]
        for h in range(B_HEADS):
            qs = slice(h * dk, (h + 1) * dk)
            vs = slice(h * dv, (h + 1) * dv)
            v_h = zb_ref[rs, 2 * B_QK_WIDTH + h * dv:2 * B_QK_WIDTH + (h + 1) * dv]
            g_h = zb_ref[rs, 2 * B_QK_WIDTH + B_V_WIDTH + h * dv:2 * B_QK_WIDTH + B_V_WIDTH + (h + 1) * dv]
            scores = _mm(q[:, qs], k[:, qs], _NT) * dmask_ref[h]
            st = state_ref[h]
            yh = _mm(scores, v_h) + _mm(qd[:, qs], st)
            state_ref[h] = st * chunk_decay[h] + _mm(kd[:, qs], v_h, _TN)
            yc = yh - jnp.mean(yh, axis=-1, keepdims=True)
            yn = yc * lax.rsqrt(jnp.mean(yc * yc, axis=-1, keepdims=True) + B_GN_EPS)
            y_ref[rs, vs] = yn * gn_ref[:, vs] * (g_h * _sigmoid(g_h))


def _retention_tables():
    heads = np.arange(B_HEADS, dtype=np.float64)
    log_gamma = np.log1p(-np.exp2(-5.0 - heads))
    pos = np.arange(B_CHUNK, dtype=np.float64)
    diff = pos[:, None] - pos[None, :]
    dmask = np.where(diff[None] >= 0, np.exp(np.maximum(diff, 0.0)[None] * log_gamma[:, None, None]), 0.0)
    q_decay = np.exp((pos + 1.0)[:, None] * log_gamma[None, :])
    k_decay = np.exp((B_CHUNK - 1.0 - pos)[:, None] * log_gamma[None, :])
    chunk_decay = tuple(float(x) for x in np.exp(B_CHUNK * log_gamma))
    rep = lambda t: np.repeat(t, B_QK_DIM, axis=1)
    return (jnp.asarray(dmask, F32), jnp.asarray(rep(q_decay), F32), jnp.asarray(rep(k_decay), F32), chunk_decay)


def _retention(zb, batch, cos, sin, gn):
    n = zb.shape[0]
    nt = n // batch // SEQ_TB
    dmask, qdec, kdec, chunk_decay = _retention_tables()
    consts = [dmask, qdec, kdec, gn.reshape(1, -1)]
    tok = lambda wd: pl.BlockSpec((SEQ_TB, wd), lambda b, t: (b * nt + t, 0))
    return pl.pallas_call(
        functools.partial(_ret_kernel, chunk_decay=chunk_decay),
        grid=(batch, nt),
        in_specs=[tok(B_IN_WIDTH), tok(LANES), tok(LANES)] + [_const_spec(c.shape) for c in consts],
        out_specs=tok(B_V_WIDTH),
        out_shape=jax.ShapeDtypeStruct((n, B_V_WIDTH), F32),
        scratch_shapes=[pltpu.VMEM((B_HEADS, B_QK_DIM, B_V_DIM), F32)],
        compiler_params=_params("parallel", "arbitrary"),
        name="retention",
    )(zb, cos, sin, *consts)


def _merge_kernel(h_ref, ya_ref, yb_ref, gp_ref, gb_ref, pa_ref, pb_ref, wo_ref, o_ref):
    d = h_ref.shape[1]
    gates = _sigmoid(gp_ref[...] + gb_ref[...])
    merged = (gates[:, :d] * _mm(ya_ref[...], pa_ref[...]) + gates[:, d:] * _mm(yb_ref[...], pb_ref[...]))
    o_ref[...] = h_ref[...] + _mm(merged, wo_ref[...])


def _merge(h, ya, yb, gp, gate_b, p_a, p_b, w_o):
    n, d = h.shape
    tok = lambda wd: pl.BlockSpec((MERGE_TM, wd), lambda i: (i, 0))
    return pl.pallas_call(
        _merge_kernel,
        grid=(n // MERGE_TM,),
        in_specs=[tok(d), tok(A_WIDTH), tok(B_V_WIDTH), tok(2 * d), _const_spec((1, 2 * d)),
                  _const_spec(p_a.shape), _const_spec(p_b.shape), _const_spec(w_o.shape)],
        out_specs=tok(d),
        out_shape=jax.ShapeDtypeStruct((n, d), F32),
        compiler_params=_params("parallel"),
        name="merge",
    )(h, ya, yb, gp, gate_b.reshape(1, 2 * d), p_a, p_b, w_o)


def _memkv_kernel(m_ref, g_ref, w_ref, o_ref):
    o_ref[...] = _mm(_rms(m_ref[...], g_ref[...]), w_ref[...]).astype(BF16)


def _memkv(mem, g, w_kv):
    n, d = mem.shape
    tm = 512
    return pl.pallas_call(
        _memkv_kernel,
        grid=(n // tm,),
        in_specs=[pl.BlockSpec((tm, d), lambda i: (i, 0)), _const_spec((1, d)), _const_spec(w_kv.shape)],
        out_specs=pl.BlockSpec((tm, 2 * d), lambda i: (i, 0)),
        out_shape=jax.ShapeDtypeStruct((n, 2 * d), BF16),
        compiler_params=_params("parallel"),
        name="memkv",
    )(mem, g.reshape(1, d), w_kv)


def _cross_kernel(h_ref, g_ref, wq_ref, kv_ref, wo_ref, o_ref):
    h = h_ref[...]
    d = h.shape[1]
    hd = d // X_HEADS
    q = _mm(_rms(h, g_ref[...]), wq_ref[...])
    outs = []
    for i in range(X_HEADS):
        s = _mm(q[:, i * hd:(i + 1) * hd], kv_ref[:, i * hd:(i + 1) * hd], _NT) * (hd ** -0.5)
        e = jnp.exp(s - jnp.max(s, axis=-1, keepdims=True))
        p = e / jnp.sum(e, axis=-1, keepdims=True)
        outs.append(_mm(p, kv_ref[:, d + i * hd:d + (i + 1) * hd]))
    o_ref[...] = h + _mm(jnp.concatenate(outs, axis=1), wo_ref[...])


def _cross(h, batch, g, w_q, kv, w_o):
    n, d = h.shape
    nt = n // batch // CROSS_TM
    mlen = kv.shape[0] // batch
    tok = pl.BlockSpec((CROSS_TM, d), lambda b, t: (b * nt + t, 0))
    return pl.pallas_call(
        _cross_kernel,
        grid=(batch, nt),
        in_specs=[tok, _const_spec((1, d)), _const_spec(w_q.shape),
                  pl.BlockSpec((mlen, 2 * d), lambda b, t: (b, 0)), _const_spec(w_o.shape)],
        out_specs=tok,
        out_shape=jax.ShapeDtypeStruct((n, d), F32),
        compiler_params=_params("parallel", "arbitrary"),
        name="cross",
    )(h, g.reshape(1, d), w_q, kv, w_o)


def kernel(x, mem, positions, norm_g, ffn_w_in, ffn_w_out, mix_w_in, mix_gate_b, shift_mu, a_w0, a_w_up, a_a0, a_a_up, a_g_up, a_k_k, a_k_a, a_r_k, a_ln, b_gn, w_branch_a, w_branch_b, mix_w_out, mem_norm, cross_w_q, cross_w_kv, cross_w_o, final_norm):
    batch, seq, d = x.shape
    depth = norm_g.shape[0]
    n = batch * seq
    assert seq % SEQ_TB == 0 and seq % CROSS_TM == 0 and n % ROPE_TM == 0
    assert SEQ_TB % B_CHUNK == 0 and SEQ_TB % RWKV_CHUNK == 0 and 2 ** RWKV_SOLVE_LEVELS == RWKV_CHUNK
    bf = lambda t: t.astype(BF16)

    half = B_QK_DIM // 2
    inv_freq = ROPE_BASE ** (-jnp.arange(half, dtype=F32) / half)
    invf = jnp.tile(inv_freq, LANES // half).reshape(1, LANES)
    lane = np.arange(LANES)
    sign = jnp.asarray(np.where((lane % B_QK_DIM) < half, -1.0, 1.0), F32).reshape(1, LANES)
    cos, sin = _rope_tables(positions.reshape(n, 1), invf, sign)

    seg = jnp.asarray(np.kron(np.eye(A_HEADS), np.ones((A_HEAD_DIM, A_HEAD_DIM))), BF16)
    tri = jnp.asarray(np.tril(np.ones((RWKV_CHUNK, RWKV_CHUNK))), BF16)
    mem2 = mem.reshape(-1, d)

    h = x.reshape(n, d)
    for l in range(depth):
        h = _ffn(h, norm_g[l, 0], bf(ffn_w_in[l, 0]), bf(ffn_w_out[l, 0]), final_norm, False)
        za, zb, gp = _inproj(h, norm_g[l, 1], bf(mix_w_in[l]))
        ya = _rwkv(za, batch, shift_mu[l], a_w0[l], bf(a_w_up[l]), a_a0[l], bf(a_a_up[l]), bf(a_g_up[l]),
                   a_k_k[l], a_k_a[l], a_r_k[l], a_ln[l], seg, tri)
        yb = _retention(zb, batch, cos, sin, b_gn[l])
        h = _merge(h, ya, yb, gp, mix_gate_b[l], bf(w_branch_a[l]), bf(w_branch_b[l]), bf(mix_w_out[l]))
        kv = _memkv(mem2, mem_norm[l], bf(cross_w_kv[l]))
        h = _cross(h, batch, norm_g[l, 2], bf(cross_w_q[l]), kv, bf(cross_w_o[l]))
        h = _ffn(h, norm_g[l, 3], bf(ffn_w_in[l, 1]), bf(ffn_w_out[l, 1]), final_norm, l == depth - 1)
    return h.reshape(batch, seq, d)
```

```python
import functools
import math

import jax
import jax.numpy as jnp
import numpy as np
from jax import lax
from jax.experimental import pallas as pl
from jax.experimental.pallas import tpu as pltpu

F32 = jnp.float32
BF16 = jnp.bfloat16

NORM_EPS = 1e-6
D_FF = 2816
A_HEADS = 8
A_HEAD_DIM = 64
A_WIDTH = A_HEADS * A_HEAD_DIM
A_RANK_W = 64
A_RANK_A = 64
A_RANK_G = 128
A_LN_EPS = 64e-5
A_SHIFT_WIDTH = 3 * A_WIDTH + A_RANK_W + A_RANK_A + A_RANK_G
B_HEADS = 8
B_QK_DIM = 64
B_V_DIM = 128
B_QK_WIDTH = B_HEADS * B_QK_DIM
B_V_WIDTH = B_HEADS * B_V_DIM
B_IN_WIDTH = 2 * B_QK_WIDTH + 2 * B_V_WIDTH
B_CHUNK = 128
B_GN_EPS = 1e-5
ROPE_BASE = 10000.0
X_HEADS = 4

RWKV_CHUNK = 64
RWKV_SOLVE_LEVELS = 6
LANES = 128
VMEM_LIMIT = 56 * 1024 * 1024

FFN_TM = 512
FFN_TF = 256
INPROJ_TM = 256
SEQ_TB = 256
MERGE_TM = 512
CROSS_TM = 512
ROPE_TM = 1024


def _const_spec(shape):
    nd = len(shape)
    return pl.BlockSpec(shape, lambda *_: (0,) * nd, pipeline_mode=pl.Buffered(1))


def _params(*sem):
    return pltpu.CompilerParams(dimension_semantics=sem, vmem_limit_bytes=VMEM_LIMIT)


def _pieces(x, n):
    if x.dtype == BF16:
        return [x]
    out, rem = [], x
    for i in range(n):
        p = rem.astype(BF16)
        out.append(p)
        if i + 1 < n:
            rem = rem - p.astype(F32)
    return out


_NN = (((1,), (0,)), ((), ()))
_NT = (((1,), (1,)), ((), ()))
_TN = (((0,), (0,)), ((), ()))


def _mm(a, b, dims=_NN, pa=1, pb=1):
    acc = None
    for x in _pieces(a, pa):
        for y in _pieces(b, pb):
            t = lax.dot_general(x, y, dims, preferred_element_type=F32)
            acc = t if acc is None else acc + t
    return acc


def _rms(x, g):
    ms = jnp.mean(x * x, axis=-1, keepdims=True)
    return x * lax.rsqrt(ms + NORM_EPS) * g


def _sigmoid(x):
    return 1.0 / (1.0 + jnp.exp(-x))


def _ffn_kernel(h_ref, g_ref, wg_ref, wu_ref, wo_ref, fg_ref, o_ref, act_ref, *, final):
    h = h_ref[...]
    u = _rms(h, g_ref[...]).astype(BF16)
    for c in range(D_FF // FFN_TF):
        sl = pl.ds(c * FFN_TF, FFN_TF)
        gate = jnp.dot(u, wg_ref[:, sl], preferred_element_type=F32)
        up = jnp.dot(u, wu_ref[:, sl], preferred_element_type=F32)
        act_ref[:, sl] = (gate * _sigmoid(gate) * up).astype(BF16)
    y = jnp.dot(act_ref[...], wo_ref[...], preferred_element_type=F32)
    out = h + 0.5 * y
    if final:
        out = _rms(out, fg_ref[...])
    o_ref[...] = out


def _ffn(h, g, w_in, w_out, final_g, final):
    n, d = h.shape
    return pl.pallas_call(
        functools.partial(_ffn_kernel, final=final),
        grid=(n // FFN_TM,),
        in_specs=[
            pl.BlockSpec((FFN_TM, d), lambda i: (i, 0)),
            _const_spec((1, d)),
            pl.BlockSpec((d, D_FF), lambda i: (0, 0), pipeline_mode=pl.Buffered(1)),
            pl.BlockSpec((d, D_FF), lambda i: (0, 1), pipeline_mode=pl.Buffered(1)),
            _const_spec((D_FF, d)),
            _const_spec((1, d)),
        ],
        out_specs=pl.BlockSpec((FFN_TM, d), lambda i: (i, 0)),
        out_shape=jax.ShapeDtypeStruct((n, d), F32),
        scratch_shapes=[pltpu.VMEM((FFN_TM, D_FF), BF16)],
        compiler_params=_params("parallel"),
        name="ffn",
    )(h, g.reshape(1, d), w_in, w_in, w_out, final_g.reshape(1, d))


def _inproj_kernel(h_ref, g_ref, w_ref, za_ref, zb_ref, gp_ref):
    u = _rms(h_ref[...], g_ref[...]).astype(BF16)
    c0, c1 = A_SHIFT_WIDTH, A_SHIFT_WIDTH + B_IN_WIDTH
    za_ref[...] = jnp.dot(u, w_ref[:, :c0], preferred_element_type=F32)
    zb_ref[...] = jnp.dot(u, w_ref[:, c0:c1], preferred_element_type=F32)
    gp_ref[...] = jnp.dot(u, w_ref[:, c1:], preferred_element_type=F32)


def _inproj(h, g, w):
    n, d = h.shape
    widths = (A_SHIFT_WIDTH, B_IN_WIDTH, w.shape[1] - A_SHIFT_WIDTH - B_IN_WIDTH)
    return pl.pallas_call(
        _inproj_kernel,
        grid=(n // INPROJ_TM,),
        in_specs=[
            pl.BlockSpec((INPROJ_TM, d), lambda i: (i, 0)),
            _const_spec((1, d)),
            _const_spec(w.shape),
        ],
        out_specs=[pl.BlockSpec((INPROJ_TM, wd), lambda i: (i, 0)) for wd in widths],
        out_shape=[jax.ShapeDtypeStruct((n, wd), F32) for wd in widths],
        compiler_params=_params("parallel"),
        name="inproj",
    )(h, g.reshape(1, d), w)


def _seg_sum(x, seg_ref):
    wd = seg_ref.shape[0]
    return jnp.concatenate([_mm(x[:, i:i + wd], seg_ref[...]) for i in range(0, x.shape[1], wd)], axis=1)


def _rwkv_kernel(za_ref, mu_ref, w0_ref, wup_ref, a0_ref, aup_ref, gup_ref, kk_ref, ka_ref, rk_ref, ln_ref,
                 seg_ref, tri_ref, y_ref, carry_ref, state_ref):
    L = RWKV_CHUNK
    hd = A_HEAD_DIM
    W2 = 2 * hd
    assert W2 == LANES and L == hd

    @pl.when(pl.program_id(1) == 0)
    def _():
        carry_ref[...] = jnp.zeros_like(carry_ref)
        state_ref[...] = jnp.zeros_like(state_ref)

    z = za_ref[...]
    tb = z.shape[0]
    row = lax.broadcasted_iota(jnp.int32, (tb, 1), 0)
    prev = jnp.where(row == 0, carry_ref[...], pltpu.roll(z, 1, 0))
    carry_ref[...] = z[tb - 1:tb, :]
    xs = z + (prev - z) * mu_ref[...]

    r = xs[:, 0:A_WIDTH]
    k = xs[:, A_WIDTH:2 * A_WIDTH]
    v = xs[:, 2 * A_WIDTH:3 * A_WIDTH]
    o = 3 * A_WIDTH
    w_lo = xs[:, o:o + A_RANK_W]
    a_lo = xs[:, o + A_RANK_W:o + A_RANK_W + A_RANK_A]
    g_lo = xs[:, o + A_RANK_W + A_RANK_A:]

    w = w0_ref[...] + _mm(jnp.tanh(w_lo), wup_ref[...])
    logd = -_sigmoid(w) * math.exp(-0.5)
    a = _sigmoid(a0_ref[...] + _mm(a_lo, aup_ref[...]))
    g = _mm(_sigmoid(g_lo), gup_ref[...])

    kk = k * kk_ref[...]
    kk = kk / jnp.maximum(jnp.sqrt(_seg_sum(kk * kk, seg_ref)), 1e-12)
    k = k * (1.0 + (a - 1.0) * ka_ref[...])
    kka = kk * a

    cum = _mm(tri_ref[...], logd, pb=2)
    tot = jnp.concatenate([jnp.broadcast_to(cum[c + L - 1:c + L, :], (L, A_WIDTH)) for c in range(0, tb, L)], axis=0)
    e_inc = jnp.exp(cum)
    e_exc = jnp.exp(cum - logd)
    e_neg = jnp.exp(-cum)
    e_end = jnp.exp(tot - cum)
    p_end = jnp.exp(tot)
    rt = r * e_inc
    at = -kk * e_exc
    bt = kka * e_neg
    kt = k * e_neg
    bh = kka * e_end
    kh = k * e_end

    def iota(shape, dim):
        return lax.broadcasted_iota(jnp.int32, shape, dim)

    first1 = (iota((L, W2), 1) & (W2 - 1)) < hd
    first2 = (iota((L, 2 * W2), 1) & (W2 - 1)) < hd
    strict_lower = (iota((L, 2 * W2), 1) & (L - 1)) < iota((L, 2 * W2), 0)
    lower = (iota((L, 2 * W2), 1) & (L - 1)) <= iota((L, 2 * W2), 0)
    eye_pair = jnp.where((iota((L, W2), 1) & (L - 1)) == iota((L, W2), 0), 1.0, 0.0)
    eye_full = iota((W2, W2), 0) == iota((W2, W2), 1)
    same_head = (iota((W2, 2 * W2), 0) < hd) == ((iota((W2, 2 * W2), 1) & (W2 - 1)) < hd)
    zeros_l = jnp.zeros((L, W2), F32)
    zeros_2l = jnp.zeros((2 * L, W2), F32)

    def stack(x):
        first = first1 if x.shape[1] == W2 else first2
        return jnp.concatenate([jnp.where(first, x, 0.0), jnp.where(first, 0.0, x)], axis=0)

    units = [(c, p) for c in range(tb // L) for p in range(A_WIDTH // W2)]

    def sl(x, u):
        c, p = u
        return x[c * L:(c + 1) * L, p * W2:(p + 1) * W2]

    a_a, a_r = {}, {}
    for u in units:
        lhs = jnp.concatenate([sl(at, u), sl(rt, u)], axis=0)
        rhs = jnp.concatenate([stack(sl(bt, u)), stack(sl(kt, u))], axis=0)
        m1 = _mm(lhs, rhs, _NT)
        a_a[u] = jnp.where(strict_lower, m1[:L], 0.0)
        a_r[u] = jnp.where(lower, m1[L:], 0.0)
    x1 = {u: _mm(a_a[u][:, W2:], stack(sl(v, u))) for u in units}

    tinv = {u: eye_pair + a_a[u][:, :W2] for u in units}
    pw = {u: _mm(a_a[u][:, :W2], stack(a_a[u][:, :W2])) for u in units}
    for lvl in range(1, RWKV_SOLVE_LEVELS):
        for u in units:
            if lvl + 1 < RWKV_SOLVE_LEVELS:
                prod = _mm(pw[u], stack(jnp.concatenate([tinv[u], pw[u]], axis=1)))
                tinv[u] = tinv[u] + prod[:, :W2]
                pw[u] = prod[:, W2:]
            else:
                tinv[u] = tinv[u] + _mm(pw[u], stack(tinv[u]))
    sol = {u: _mm(tinv[u], stack(jnp.concatenate([sl(at, u), x1[u]], axis=1))) for u in units}

    qp, yp, gbd, hp = {}, {}, {}, {}
    for u in units:
        v_u = sl(v, u)
        rhs_a = jnp.concatenate([stack(sol[u]), jnp.concatenate([zeros_2l, stack(v_u)], axis=1)], axis=0)
        m2a = _mm(a_r[u], rhs_a)
        rhs_b = jnp.concatenate([sol[u], jnp.concatenate([zeros_l, v_u], axis=1)], axis=0)
        m2b = _mm(jnp.concatenate([sl(bh, u), sl(kh, u)], axis=0), rhs_b, _TN)
        m2b = jnp.where(same_head, m2b, 0.0)
        qp[u] = sl(rt, u) + m2a[:, :W2]
        yp[u] = m2a[:, W2:]
        c, p = u
        gbd[u] = jnp.where(eye_full, p_end[c * L:c * L + 1, p * W2:(p + 1) * W2], 0.0) + m2b[:, :W2]
        hp[u] = m2b[:, W2:]

    for u in units:
        c, p = u
        m3 = _mm(jnp.concatenate([qp[u], gbd[u]], axis=0), state_ref[p])
        y_ref[c * L:(c + 1) * L, p * W2:(p + 1) * W2] = m3[:L] + yp[u]
        state_ref[p] = m3[L:] + hp[u]

    y = y_ref[...]
    inv_n = 1.0 / hd
    mean = _seg_sum(y, seg_ref) * inv_n
    yc = y - mean
    var = _seg_sum(yc * yc, seg_ref) * inv_n
    yn = yc * lax.rsqrt(var + A_LN_EPS) * ln_ref[0:1, :] + ln_ref[1:2, :]
    bonus = _seg_sum(r * k * rk_ref[...], seg_ref) * v
    y_ref[...] = (yn + bonus) * g


def _rwkv(za, batch, mu, w0, w_up, a0, a_up, g_up, k_k, k_a, r_k, ln, seg, tri):
    n = za.shape[0]
    nt = n // batch // SEQ_TB
    row = lambda p: p.reshape(1, -1)
    consts = [row(mu), row(w0), w_up, row(a0), a_up, g_up, row(k_k), row(k_a), row(r_k), ln, seg, tri]
    return pl.pallas_call(
        _rwkv_kernel,
        grid=(batch, nt),
        in_specs=[pl.BlockSpec((SEQ_TB, A_SHIFT_WIDTH), lambda b, t: (b * nt + t, 0))]
        + [_const_spec(c.shape) for c in consts],
        out_specs=pl.BlockSpec((SEQ_TB, A_WIDTH), lambda b, t: (b * nt + t, 0)),
        out_shape=jax.ShapeDtypeStruct((n, A_WIDTH), F32),
        scratch_shapes=[pltpu.VMEM((1, A_SHIFT_WIDTH), F32),
                        pltpu.VMEM((A_HEADS // 2, 2 * A_HEAD_DIM, 2 * A_HEAD_DIM), F32)],
        compiler_params=_params("parallel", "arbitrary"),
        name="rwkv",
    )(za, *consts)


def _rope_kernel(pos_ref, invf_ref, sign_ref, cos_ref, sin_ref):
    ang = pos_ref[...].astype(F32) * invf_ref[...]
    cos_ref[...] = jnp.cos(ang)
    sin_ref[...] = jnp.sin(ang) * sign_ref[...]


def _rope_tables(pos, invf, sign):
    n = pos.shape[0]
    return pl.pallas_call(
        _rope_kernel,
        grid=(n // ROPE_TM,),
        in_specs=[pl.BlockSpec((ROPE_TM, 1), lambda i: (i, 0)), _const_spec((1, LANES)), _const_spec((1, LANES))],
        out_specs=[pl.BlockSpec((ROPE_TM, LANES), lambda i: (i, 0))] * 2,
        out_shape=[jax.ShapeDtypeStruct((n, LANES), F32)] * 2,
        compiler_params=_params("parallel"),
        name="rope",
    )(pos, invf, sign)


def _ret_kernel(zb_ref, cos_ref, sin_ref, dmask_ref, qdec_ref, kdec_ref, cdec_ref, gn_ref, segm_ref, y_ref, state_ref):
    C = B_CHUNK
    dk, dv = B_QK_DIM, B_V_DIM
    QW, VW = 2 * dk, 2 * dv
    assert QW == LANES and dv == LANES

    @pl.when(pl.program_id(1) == 0)
    def _():
        state_ref[...] = jnp.zeros_like(state_ref)

    def iota(shape, dim):
        return lax.broadcasted_iota(jnp.int32, shape, dim)

    first_half = (iota((C, B_QK_WIDTH), 1) & (dk - 1)) < dk // 2
    first_q = iota((C, QW), 1) < dk
    first_v = iota((C, VW), 1) < dv
    same_head = (iota((QW, VW), 0) < dk) == (iota((QW, VW), 1) < dv)

    def rotate(x, cos, sin):
        partner = jnp.where(first_half, pltpu.roll(x, B_QK_WIDTH - dk // 2, 1), pltpu.roll(x, dk // 2, 1))
        return x * cos + partner * sin

    def stack(x, first):
        return jnp.concatenate([jnp.where(first, x, 0.0), jnp.where(first, 0.0, x)], axis=0)

    chunks = range(zb_ref.shape[0] // C)
    q, k, qd, kd = {}, {}, {}, {}
    for c in chunks:
        rs = slice(c * C, (c + 1) * C)
        reps = B_QK_WIDTH // LANES
        cos = jnp.concatenate([cos_ref[rs, :]] * reps, axis=1)
        sin = jnp.concatenate([sin_ref[rs, :]] * reps, axis=1)
        q[c] = rotate(zb_ref[rs, 0:B_QK_WIDTH], cos, sin)
        k[c] = rotate(zb_ref[rs, B_QK_WIDTH:2 * B_QK_WIDTH], cos, sin) * (dk ** -0.5)
        qd[c] = q[c] * qdec_ref[...]
        kd[c] = k[c] * kdec_ref[...]

    units = [(c, p) for c in chunks for p in range(B_HEADS // 2)]

    def qk(x, u):
        return x[u[0]][:, u[1] * QW:(u[1] + 1) * QW]

    def vg(base, u):
        c, p = u
        return zb_ref[c * C:(c + 1) * C, base + p * VW:base + (p + 1) * VW]

    v = {u: vg(2 * B_QK_WIDTH, u) for u in units}
    scores = {u: _mm(qk(q, u), stack(qk(k, u), first_q), _NT) * dmask_ref[u[1]] for u in units}
    inner = {u: _mm(scores[u], stack(v[u], first_v)) for u in units}
    kv = {u: jnp.where(same_head, _mm(qk(kd, u), v[u], _TN), 0.0) for u in units}

    before = {}
    for p in range(B_HEADS // 2):
        st = state_ref[p]
        for c in chunks:
            before[(c, p)] = st
            st = st * cdec_ref[p] + kv[(c, p)]
        state_ref[p] = st

    for u in units:
        c, p = u
        y = inner[u] + _mm(qk(qd, u), before[u])
        yc = y - _mm(y, segm_ref[...])
        yn = yc * lax.rsqrt(_mm(yc * yc, segm_ref[...]) + B_GN_EPS)
        g = vg(2 * B_QK_WIDTH + B_V_WIDTH, u)
        y_ref[c * C:(c + 1) * C, p * VW:(p + 1) * VW] = yn * gn_ref[:, p * VW:(p + 1) * VW] * (g * _sigmoid(g))


def _retention_tables():
    heads = np.arange(B_HEADS, dtype=np.float64)
    log_gamma = np.log1p(-np.exp2(-5.0 - heads))
    pos = np.arange(B_CHUNK, dtype=np.float64)
    diff = pos[:, None] - pos[None, :]
    dmask = np.where(diff[None] >= 0, np.exp(np.maximum(diff, 0.0)[None] * log_gamma[:, None, None]), 0.0)
    dmask = dmask.reshape(B_HEADS // 2, 2, B_CHUNK, B_CHUNK).transpose(0, 2, 1, 3).reshape(B_HEADS // 2, B_CHUNK, -1)
    q_decay = np.exp((pos + 1.0)[:, None] * log_gamma[None, :])
    k_decay = np.exp((B_CHUNK - 1.0 - pos)[:, None] * log_gamma[None, :])
    chunk_decay = np.exp(B_CHUNK * log_gamma).reshape(B_HEADS // 2, 2, 1, 1)
    cdec = np.broadcast_to(chunk_decay, (B_HEADS // 2, 2, B_QK_DIM, 2 * B_V_DIM)).reshape(B_HEADS // 2, 2 * B_QK_DIM, -1)
    segm = np.kron(np.eye(2), np.full((B_V_DIM, B_V_DIM), 1.0 / B_V_DIM))
    rep = lambda t: np.repeat(t, B_QK_DIM, axis=1)
    return (jnp.asarray(dmask, F32), jnp.asarray(rep(q_decay), F32), jnp.asarray(rep(k_decay), F32),
            jnp.asarray(cdec, F32), jnp.asarray(segm, BF16))


def _retention(zb, batch, cos, sin, gn):
    n = zb.shape[0]
    nt = n // batch // SEQ_TB
    dmask, qdec, kdec, cdec, segm = _retention_tables()
    consts = [dmask, qdec, kdec, cdec, gn.reshape(1, -1), segm]
    tok = lambda wd: pl.BlockSpec((SEQ_TB, wd), lambda b, t: (b * nt + t, 0))
    return pl.pallas_call(
        _ret_kernel,
        grid=(batch, nt),
        in_specs=[tok(B_IN_WIDTH), tok(LANES), tok(LANES)] + [_const_spec(c.shape) for c in consts],
        out_specs=tok(B_V_WIDTH),
        out_shape=jax.ShapeDtypeStruct((n, B_V_WIDTH), F32),
        scratch_shapes=[pltpu.VMEM((B_HEADS // 2, 2 * B_QK_DIM, 2 * B_V_DIM), F32)],
        compiler_params=_params("parallel", "arbitrary"),
        name="retention",
    )(zb, cos, sin, *consts)


def _merge_kernel(h_ref, ya_ref, yb_ref, gp_ref, gb_ref, pa_ref, pb_ref, wo_ref, o_ref):
    d = h_ref.shape[1]
    gates = _sigmoid(gp_ref[...] + gb_ref[...])
    merged = (gates[:, :d] * _mm(ya_ref[...], pa_ref[...]) + gates[:, d:] * _mm(yb_ref[...], pb_ref[...]))
    o_ref[...] = h_ref[...] + _mm(merged, wo_ref[...])


def _merge(h, ya, yb, gp, gate_b, p_a, p_b, w_o):
    n, d = h.shape
    tok = lambda wd: pl.BlockSpec((MERGE_TM, wd), lambda i: (i, 0))
    return pl.pallas_call(
        _merge_kernel,
        grid=(n // MERGE_TM,),
        in_specs=[tok(d), tok(A_WIDTH), tok(B_V_WIDTH), tok(2 * d), _const_spec((1, 2 * d)),
                  _const_spec(p_a.shape), _const_spec(p_b.shape), _const_spec(w_o.shape)],
        out_specs=tok(d),
        out_shape=jax.ShapeDtypeStruct((n, d), F32),
        compiler_params=_params("parallel"),
        name="merge",
    )(h, ya, yb, gp, gate_b.reshape(1, 2 * d), p_a, p_b, w_o)


def _memkv_kernel(m_ref, g_ref, w_ref, o_ref):
    o_ref[...] = _mm(_rms(m_ref[...], g_ref[...]), w_ref[...]).astype(BF16)


def _memkv(mem, g, w_kv):
    n, d = mem.shape
    tm = 512
    return pl.pallas_call(
        _memkv_kernel,
        grid=(n // tm,),
        in_specs=[pl.BlockSpec((tm, d), lambda i: (i, 0)), _const_spec((1, d)), _const_spec(w_kv.shape)],
        out_specs=pl.BlockSpec((tm, 2 * d), lambda i: (i, 0)),
        out_shape=jax.ShapeDtypeStruct((n, 2 * d), BF16),
        compiler_params=_params("parallel"),
        name="memkv",
    )(mem, g.reshape(1, d), w_kv)


def _cross_kernel(h_ref, g_ref, wq_ref, kv_ref, wo_ref, o_ref):
    h = h_ref[...]
    d = h.shape[1]
    hd = d // X_HEADS
    q = _mm(_rms(h, g_ref[...]), wq_ref[...])
    outs = []
    for i in range(X_HEADS):
        s = _mm(q[:, i * hd:(i + 1) * hd], kv_ref[:, i * hd:(i + 1) * hd], _NT) * (hd ** -0.5)
        e = jnp.exp(s - jnp.max(s, axis=-1, keepdims=True))
        p = e / jnp.sum(e, axis=-1, keepdims=True)
        outs.append(_mm(p, kv_ref[:, d + i * hd:d + (i + 1) * hd]))
    o_ref[...] = h + _mm(jnp.concatenate(outs, axis=1), wo_ref[...])


def _cross(h, batch, g, w_q, kv, w_o):
    n, d = h.shape
    nt = n // batch // CROSS_TM
    mlen = kv.shape[0] // batch
    tok = pl.BlockSpec((CROSS_TM, d), lambda b, t: (b * nt + t, 0))
    return pl.pallas_call(
        _cross_kernel,
        grid=(batch, nt),
        in_specs=[tok, _const_spec((1, d)), _const_spec(w_q.shape),
                  pl.BlockSpec((mlen, 2 * d), lambda b, t: (b, 0)), _const_spec(w_o.shape)],
        out_specs=tok,
        out_shape=jax.ShapeDtypeStruct((n, d), F32),
        compiler_params=_params("parallel", "arbitrary"),
        name="cross",
    )(h, g.reshape(1, d), w_q, kv, w_o)


def kernel(x, mem, positions, norm_g, ffn_w_in, ffn_w_out, mix_w_in, mix_gate_b, shift_mu, a_w0, a_w_up, a_a0, a_a_up, a_g_up, a_k_k, a_k_a, a_r_k, a_ln, b_gn, w_branch_a, w_branch_b, mix_w_out, mem_norm, cross_w_q, cross_w_kv, cross_w_o, final_norm):
    batch, seq, d = x.shape
    depth = norm_g.shape[0]
    n = batch * seq
    assert seq % SEQ_TB == 0 and seq % CROSS_TM == 0 and n % ROPE_TM == 0
    assert SEQ_TB % B_CHUNK == 0 and SEQ_TB % RWKV_CHUNK == 0 and 2 ** RWKV_SOLVE_LEVELS == RWKV_CHUNK
    bf = lambda t: t.astype(BF16)

    half = B_QK_DIM // 2
    inv_freq = ROPE_BASE ** (-jnp.arange(half, dtype=F32) / half)
    invf = jnp.tile(inv_freq, LANES // half).reshape(1, LANES)
    lane = np.arange(LANES)
    sign = jnp.asarray(np.where((lane % B_QK_DIM) < half, -1.0, 1.0), F32).reshape(1, LANES)
    cos, sin = _rope_tables(positions.reshape(n, 1), invf, sign)

    seg = jnp.asarray(np.kron(np.eye(LANES // A_HEAD_DIM), np.ones((A_HEAD_DIM, A_HEAD_DIM))), BF16)
    tri = jnp.asarray(np.kron(np.eye(SEQ_TB // RWKV_CHUNK), np.tril(np.ones((RWKV_CHUNK, RWKV_CHUNK)))), BF16)
    mem2 = mem.reshape(-1, d)

    h = x.reshape(n, d)
    for l in range(depth):
        h = _ffn(h, norm_g[l, 0], bf(ffn_w_in[l, 0]), bf(ffn_w_out[l, 0]), final_norm, False)
        za, zb, gp = _inproj(h, norm_g[l, 1], bf(mix_w_in[l]))
        ya = _rwkv(za, batch, shift_mu[l], a_w0[l], bf(a_w_up[l]), a_a0[l], bf(a_a_up[l]), bf(a_g_up[l]),
                   a_k_k[l], a_k_a[l], a_r_k[l], a_ln[l], seg, tri)
        yb = _retention(zb, batch, cos, sin, b_gn[l])
        h = _merge(h, ya, yb, gp, mix_gate_b[l], bf(w_branch_a[l]), bf(w_branch_b[l]), bf(mix_w_out[l]))
        kv = _memkv(mem2, mem_norm[l], bf(cross_w_kv[l]))
        h = _cross(h, batch, norm_g[l, 2], bf(cross_w_q[l]), kv, bf(cross_w_o[l]))
        h = _ffn(h, norm_g[l, 3], bf(ffn_w_in[l, 1]), bf(ffn_w_out[l, 1]), final_norm, l == depth - 1)
    return h.reshape(batch, seq, d)
```

```python
import functools
import math

import jax
import jax.numpy as jnp
import numpy as np
from jax import lax
from jax.experimental import pallas as pl
from jax.experimental.pallas import tpu as pltpu

F32 = jnp.float32
BF16 = jnp.bfloat16

NORM_EPS = 1e-6
D_FF = 2816
A_HEADS = 8
A_HEAD_DIM = 64
A_WIDTH = A_HEADS * A_HEAD_DIM
A_RANK_W = 64
A_RANK_A = 64
A_RANK_G = 128
A_LN_EPS = 64e-5
A_SHIFT_WIDTH = 3 * A_WIDTH + A_RANK_W + A_RANK_A + A_RANK_G
B_HEADS = 8
B_QK_DIM = 64
B_V_DIM = 128
B_QK_WIDTH = B_HEADS * B_QK_DIM
B_V_WIDTH = B_HEADS * B_V_DIM
B_IN_WIDTH = 2 * B_QK_WIDTH + 2 * B_V_WIDTH
B_CHUNK = 128
B_GN_EPS = 1e-5
ROPE_BASE = 10000.0
X_HEADS = 4

RWKV_CHUNK = 64
RWKV_SOLVE_LEVELS = 6
LANES = 128
VMEM_LIMIT = 56 * 1024 * 1024

FFN_TM = 512
FFN_TF = 256
INPROJ_TM = 256
SEQ_TB = 256
MERGE_TM = 512
CROSS_TM = 512
ROPE_TM = 1024


def _const_spec(shape):
    nd = len(shape)
    return pl.BlockSpec(shape, lambda *_: (0,) * nd, pipeline_mode=pl.Buffered(1))


def _params(*sem):
    return pltpu.CompilerParams(dimension_semantics=sem, vmem_limit_bytes=VMEM_LIMIT)


def _pieces(x, n):
    if x.dtype == BF16:
        return [x]
    out, rem = [], x
    for i in range(n):
        p = rem.astype(BF16)
        out.append(p)
        if i + 1 < n:
            rem = rem - p.astype(F32)
    return out


_NN = (((1,), (0,)), ((), ()))
_NT = (((1,), (1,)), ((), ()))
_TN = (((0,), (0,)), ((), ()))


def _mm(a, b, dims=_NN, pa=1, pb=1):
    acc = None
    for x in _pieces(a, pa):
        for y in _pieces(b, pb):
            t = lax.dot_general(x, y, dims, preferred_element_type=F32)
            acc = t if acc is None else acc + t
    return acc


def _rms(x, g):
    ms = jnp.mean(x * x, axis=-1, keepdims=True)
    return x * lax.rsqrt(ms + NORM_EPS) * g


def _sigmoid(x):
    return 0.5 * jnp.tanh(0.5 * x) + 0.5


def _ffn_kernel(h_ref, g_ref, wg_ref, wu_ref, wo_ref, fg_ref, o_ref, act_ref, *, final):
    h = h_ref[...]
    u = _rms(h, g_ref[...]).astype(BF16)
    for c in range(D_FF // FFN_TF):
        sl = pl.ds(c * FFN_TF, FFN_TF)
        gate = jnp.dot(u, wg_ref[:, sl], preferred_element_type=F32)
        up = jnp.dot(u, wu_ref[:, sl], preferred_element_type=F32)
        act_ref[:, sl] = (gate * _sigmoid(gate) * up).astype(BF16)
    y = jnp.dot(act_ref[...], wo_ref[...], preferred_element_type=F32)
    out = h + 0.5 * y
    if final:
        out = _rms(out, fg_ref[...])
    o_ref[...] = out


def _ffn(h, g, w_in, w_out, final_g, final):
    n, d = h.shape
    return pl.pallas_call(
        functools.partial(_ffn_kernel, final=final),
        grid=(n // FFN_TM,),
        in_specs=[
            pl.BlockSpec((FFN_TM, d), lambda i: (i, 0)),
            _const_spec((1, d)),
            pl.BlockSpec((d, D_FF), lambda i: (0, 0), pipeline_mode=pl.Buffered(1)),
            pl.BlockSpec((d, D_FF), lambda i: (0, 1), pipeline_mode=pl.Buffered(1)),
            _const_spec((D_FF, d)),
            _const_spec((1, d)),
        ],
        out_specs=pl.BlockSpec((FFN_TM, d), lambda i: (i, 0)),
        out_shape=jax.ShapeDtypeStruct((n, d), F32),
        scratch_shapes=[pltpu.VMEM((FFN_TM, D_FF), BF16)],
        compiler_params=_params("parallel"),
        name="ffn",
    )(h, g.reshape(1, d), w_in, w_in, w_out, final_g.reshape(1, d))


def _inproj_kernel(h_ref, g_ref, w_ref, za_ref, zb_ref, gp_ref):
    u = _rms(h_ref[...], g_ref[...]).astype(BF16)
    c0, c1 = A_SHIFT_WIDTH, A_SHIFT_WIDTH + B_IN_WIDTH
    za_ref[...] = jnp.dot(u, w_ref[:, :c0], preferred_element_type=F32)
    zb_ref[...] = jnp.dot(u, w_ref[:, c0:c1], preferred_element_type=F32)
    gp_ref[...] = jnp.dot(u, w_ref[:, c1:], preferred_element_type=F32).astype(gp_ref.dtype)


def _inproj(h, g, w):
    n, d = h.shape
    widths = (A_SHIFT_WIDTH, B_IN_WIDTH, w.shape[1] - A_SHIFT_WIDTH - B_IN_WIDTH)
    return pl.pallas_call(
        _inproj_kernel,
        grid=(n // INPROJ_TM,),
        in_specs=[
            pl.BlockSpec((INPROJ_TM, d), lambda i: (i, 0)),
            _const_spec((1, d)),
            _const_spec(w.shape),
        ],
        out_specs=[pl.BlockSpec((INPROJ_TM, wd), lambda i: (i, 0)) for wd in widths],
        out_shape=[jax.ShapeDtypeStruct((n, wd), dt) for wd, dt in zip(widths, (F32, F32, BF16))],
        compiler_params=_params("parallel"),
        name="inproj",
    )(h, g.reshape(1, d), w)


def _seg_sum(x, seg_ref):
    wd = seg_ref.shape[0]
    return jnp.concatenate([_mm(x[:, i:i + wd], seg_ref[...]) for i in range(0, x.shape[1], wd)], axis=1)


def _seg_sum_two(x1, x2, seg2_ref):
    wd = seg2_ref.shape[0] // 2
    o1, o2 = [], []
    for i in range(0, x1.shape[1], wd):
        s = _mm(jnp.concatenate([x1[:, i:i + wd], x2[:, i:i + wd]], axis=1), seg2_ref[...])
        o1.append(s[:, :wd])
        o2.append(s[:, wd:])
    return jnp.concatenate(o1, axis=1), jnp.concatenate(o2, axis=1)


def _rwkv_kernel(za_ref, mu_ref, w0_ref, wup_ref, a0_ref, aup_ref, gup_ref, kk_ref, ka_ref, rk_ref, ln_ref,
                 seg_ref, seg2_ref, tri_ref, y_ref, carry_ref, state_ref):
    L = RWKV_CHUNK
    hd = A_HEAD_DIM
    W2 = 2 * hd
    assert W2 == LANES and L == hd

    @pl.when(pl.program_id(1) == 0)
    def _():
        carry_ref[...] = jnp.zeros_like(carry_ref)
        state_ref[...] = jnp.zeros_like(state_ref)

    z = za_ref[...]
    tb = z.shape[0]
    row = lax.broadcasted_iota(jnp.int32, (tb, 1), 0)
    prev = jnp.where(row == 0, carry_ref[...], pltpu.roll(z, 1, 0))
    carry_ref[...] = z[tb - 1:tb, :]
    xs = z + (prev - z) * mu_ref[...]

    r = xs[:, 0:A_WIDTH]
    k = xs[:, A_WIDTH:2 * A_WIDTH]
    v = xs[:, 2 * A_WIDTH:3 * A_WIDTH]
    o = 3 * A_WIDTH
    w_lo = xs[:, o:o + A_RANK_W]
    a_lo = xs[:, o + A_RANK_W:o + A_RANK_W + A_RANK_A]
    g_lo = xs[:, o + A_RANK_W + A_RANK_A:]

    w = w0_ref[...] + _mm(jnp.tanh(w_lo), wup_ref[...])
    logd = -_sigmoid(w) * math.exp(-0.5)
    a = _sigmoid(a0_ref[...] + _mm(a_lo, aup_ref[...]))
    g = _mm(_sigmoid(g_lo), gup_ref[...])

    kk = k * kk_ref[...]
    kk = kk * lax.rsqrt(jnp.maximum(_seg_sum(kk * kk, seg_ref), 1e-24))
    k = k * (1.0 + (a - 1.0) * ka_ref[...])
    kka = kk * a

    cum = _mm(tri_ref[...], logd, pb=2)
    tot = jnp.concatenate([jnp.broadcast_to(cum[c + L - 1:c + L, :], (L, A_WIDTH)) for c in range(0, tb, L)], axis=0)
    e_inc = jnp.exp(cum)
    e_exc = jnp.exp(cum - logd)
    e_neg = jnp.exp(-cum)
    e_end = jnp.exp(tot - cum)
    p_end = jnp.exp(tot)
    rt = r * e_inc
    at = -kk * e_exc
    bt = kka * e_neg
    kt = k * e_neg
    bh = kka * e_end
    kh = k * e_end

    def iota(shape, dim):
        return lax.broadcasted_iota(jnp.int32, shape, dim)

    first1 = (iota((L, W2), 1) & (W2 - 1)) < hd
    first2 = (iota((L, 2 * W2), 1) & (W2 - 1)) < hd
    strict_lower = (iota((L, 2 * W2), 1) & (L - 1)) < iota((L, 2 * W2), 0)
    lower = (iota((L, 2 * W2), 1) & (L - 1)) <= iota((L, 2 * W2), 0)
    eye_pair = jnp.where((iota((L, W2), 1) & (L - 1)) == iota((L, W2), 0), 1.0, 0.0)
    eye_full = iota((W2, W2), 0) == iota((W2, W2), 1)
    same_head = (iota((W2, 2 * W2), 0) < hd) == ((iota((W2, 2 * W2), 1) & (W2 - 1)) < hd)
    zeros_l = jnp.zeros((L, W2), BF16)
    zeros_2l = jnp.zeros((2 * L, W2), BF16)

    def stack(x):
        first = first1 if x.shape[1] == W2 else first2
        return jnp.concatenate([jnp.where(first, x, 0.0), jnp.where(first, 0.0, x)], axis=0)

    units = [(c, p) for c in range(tb // L) for p in range(A_WIDTH // W2)]

    def sl(x, u):
        c, p = u
        return x[c * L:(c + 1) * L, p * W2:(p + 1) * W2]

    at_b, rt_b, bt_b, kt_b, bh_b, kh_b, v_b = (x.astype(BF16) for x in (at, rt, bt, kt, bh, kh, v))

    a_ab, a_ak, a_r = {}, {}, {}
    for u in units:
        lhs = jnp.concatenate([sl(at_b, u), sl(rt_b, u)], axis=0)
        rhs = jnp.concatenate([stack(sl(bt_b, u)), stack(sl(kt_b, u))], axis=0)
        m1 = _mm(lhs, rhs, _NT)
        a_a = jnp.where(strict_lower, m1[:L], 0.0)
        a_ab[u] = a_a[:, :W2]
        a_ak[u] = a_a[:, W2:].astype(BF16)
        a_r[u] = jnp.where(lower, m1[L:], 0.0).astype(BF16)
    x1 = {u: _mm(a_ak[u], stack(sl(v_b, u))) for u in units}

    tinv = {u: eye_pair + a_ab[u] for u in units}
    pw = {u: a_ab[u].astype(BF16) for u in units}
    pw = {u: _mm(pw[u], stack(pw[u])).astype(BF16) for u in units}
    for lvl in range(1, RWKV_SOLVE_LEVELS):
        for u in units:
            t_b = tinv[u].astype(BF16)
            if lvl + 1 < RWKV_SOLVE_LEVELS:
                prod = _mm(pw[u], stack(jnp.concatenate([t_b, pw[u]], axis=1)))
                tinv[u] = tinv[u] + prod[:, :W2]
                pw[u] = prod[:, W2:].astype(BF16)
            else:
                tinv[u] = tinv[u] + _mm(pw[u], stack(t_b))
    sol = {u: _mm(tinv[u], stack(jnp.concatenate([sl(at_b, u), x1[u].astype(BF16)], axis=1))).astype(BF16)
           for u in units}

    qp, yp, gbd, hp = {}, {}, {}, {}
    for u in units:
        v_u = sl(v_b, u)
        rhs_a = jnp.concatenate([stack(sol[u]), jnp.concatenate([zeros_2l, stack(v_u)], axis=1)], axis=0)
        m2a = _mm(a_r[u], rhs_a)
        rhs_b = jnp.concatenate([sol[u], jnp.concatenate([zeros_l, v_u], axis=1)], axis=0)
        m2b = _mm(jnp.concatenate([sl(bh_b, u), sl(kh_b, u)], axis=0), rhs_b, _TN)
        m2b = jnp.where(same_head, m2b, 0.0)
        qp[u] = sl(rt, u) + m2a[:, :W2]
        yp[u] = m2a[:, W2:]
        c, p = u
        gbd[u] = jnp.where(eye_full, p_end[c * L:c * L + 1, p * W2:(p + 1) * W2], 0.0) + m2b[:, :W2]
        hp[u] = m2b[:, W2:]

    ys = {}
    for u in units:
        m3 = _mm(jnp.concatenate([qp[u], gbd[u]], axis=0), state_ref[u[1]])
        ys[u] = m3[:L] + yp[u]
        state_ref[u[1]] = m3[L:] + hp[u]

    y = jnp.concatenate([jnp.concatenate([ys[(c, p)] for p in range(A_WIDTH // W2)], axis=1)
                         for c in range(tb // L)], axis=0)
    inv_n = 1.0 / hd
    y_sum, rk_sum = _seg_sum_two(y, r * k * rk_ref[...], seg2_ref)
    yc = y - y_sum * inv_n
    var = _seg_sum(yc * yc, seg_ref) * inv_n
    yn = yc * lax.rsqrt(var + A_LN_EPS) * ln_ref[0:1, :] + ln_ref[1:2, :]
    y_ref[...] = ((yn + rk_sum * v) * g).astype(y_ref.dtype)


def _rwkv(za, batch, mu, w0, w_up, a0, a_up, g_up, k_k, k_a, r_k, ln, seg, seg2, tri):
    n = za.shape[0]
    nt = n // batch // SEQ_TB
    row = lambda p: p.reshape(1, -1)
    consts = [row(mu), row(w0), w_up, row(a0), a_up, g_up, row(k_k), row(k_a), row(r_k), ln, seg, seg2, tri]
    return pl.pallas_call(
        _rwkv_kernel,
        grid=(batch, nt),
        in_specs=[pl.BlockSpec((SEQ_TB, A_SHIFT_WIDTH), lambda b, t: (b * nt + t, 0))]
        + [_const_spec(c.shape) for c in consts],
        out_specs=pl.BlockSpec((SEQ_TB, A_WIDTH), lambda b, t: (b * nt + t, 0)),
        out_shape=jax.ShapeDtypeStruct((n, A_WIDTH), BF16),
        scratch_shapes=[pltpu.VMEM((1, A_SHIFT_WIDTH), F32),
                        pltpu.VMEM((A_HEADS // 2, 2 * A_HEAD_DIM, 2 * A_HEAD_DIM), F32)],
        compiler_params=_params("parallel", "arbitrary"),
        name="rwkv",
    )(za, *consts)


def _rope_kernel(pos_ref, invf_ref, sign_ref, cos_ref, sin_ref):
    ang = pos_ref[...].astype(F32) * invf_ref[...]
    cos_ref[...] = jnp.cos(ang)
    sin_ref[...] = jnp.sin(ang) * sign_ref[...]


def _rope_tables(pos, invf, sign):
    n = pos.shape[0]
    return pl.pallas_call(
        _rope_kernel,
        grid=(n // ROPE_TM,),
        in_specs=[pl.BlockSpec((ROPE_TM, 1), lambda i: (i, 0)), _const_spec((1, LANES)), _const_spec((1, LANES))],
        out_specs=[pl.BlockSpec((ROPE_TM, LANES), lambda i: (i, 0))] * 2,
        out_shape=[jax.ShapeDtypeStruct((n, LANES), F32)] * 2,
        compiler_params=_params("parallel"),
        name="rope",
    )(pos, invf, sign)


def _ret_kernel(zb_ref, cos_ref, sin_ref, dmask_ref, qdec_ref, kdec_ref, cdec_ref, gn_ref, segm_ref, y_ref, state_ref):
    C = B_CHUNK
    dk, dv = B_QK_DIM, B_V_DIM
    QW, VW = 2 * dk, 2 * dv
    assert QW == LANES and dv == LANES

    @pl.when(pl.program_id(1) == 0)
    def _():
        state_ref[...] = jnp.zeros_like(state_ref)

    def iota(shape, dim):
        return lax.broadcasted_iota(jnp.int32, shape, dim)

    first_half = (iota((C, B_QK_WIDTH), 1) & (dk - 1)) < dk // 2
    first_q = iota((C, QW), 1) < dk
    first_v = iota((C, VW), 1) < dv
    same_head = (iota((QW, VW), 0) < dk) == (iota((QW, VW), 1) < dv)

    def rotate(x, cos, sin):
        partner = jnp.where(first_half, pltpu.roll(x, B_QK_WIDTH - dk // 2, 1), pltpu.roll(x, dk // 2, 1))
        return x * cos + partner * sin

    def stack(x, first):
        return jnp.concatenate([jnp.where(first, x, 0.0), jnp.where(first, 0.0, x)], axis=0)

    chunks = range(zb_ref.shape[0] // C)
    q, k, qd, kd = {}, {}, {}, {}
    for c in chunks:
        rs = slice(c * C, (c + 1) * C)
        reps = B_QK_WIDTH // LANES
        cos = jnp.concatenate([cos_ref[rs, :]] * reps, axis=1)
        sin = jnp.concatenate([sin_ref[rs, :]] * reps, axis=1)
        q[c] = rotate(zb_ref[rs, 0:B_QK_WIDTH], cos, sin)
        k[c] = rotate(zb_ref[rs, B_QK_WIDTH:2 * B_QK_WIDTH], cos, sin) * (dk ** -0.5)
        qd[c] = q[c] * qdec_ref[...]
        kd[c] = k[c] * kdec_ref[...]

    units = [(c, p) for c in chunks for p in range(B_HEADS // 2)]

    def qk(x, u):
        return x[u[0]][:, u[1] * QW:(u[1] + 1) * QW]

    def vg(base, u):
        c, p = u
        return zb_ref[c * C:(c + 1) * C, base + p * VW:base + (p + 1) * VW]

    v = {u: vg(2 * B_QK_WIDTH, u) for u in units}
    scores = {u: _mm(qk(q, u), stack(qk(k, u), first_q), _NT) * dmask_ref[u[1]] for u in units}
    inner = {u: _mm(scores[u], stack(v[u], first_v)) for u in units}
    kv = {u: jnp.where(same_head, _mm(qk(kd, u), v[u], _TN), 0.0) for u in units}

    before = {}
    for p in range(B_HEADS // 2):
        st = state_ref[p]
        for c in chunks:
            before[(c, p)] = st
            st = st * cdec_ref[p] + kv[(c, p)]
        state_ref[p] = st

    y = {u: inner[u] + _mm(qk(qd, u), before[u]) for u in units}
    yc = {u: y[u] - _mm(y[u], segm_ref[...]) for u in units}
    var = {u: _mm(yc[u] * yc[u], segm_ref[...]) for u in units}
    for u in units:
        c, p = u
        g = vg(2 * B_QK_WIDTH + B_V_WIDTH, u)
        out = yc[u] * lax.rsqrt(var[u] + B_GN_EPS) * gn_ref[:, p * VW:(p + 1) * VW] * (g * _sigmoid(g))
        y_ref[c * C:(c + 1) * C, p * VW:(p + 1) * VW] = out.astype(y_ref.dtype)


def _retention_tables():
    heads = np.arange(B_HEADS, dtype=np.float64)
    log_gamma = np.log1p(-np.exp2(-5.0 - heads))
    pos = np.arange(B_CHUNK, dtype=np.float64)
    diff = pos[:, None] - pos[None, :]
    dmask = np.where(diff[None] >= 0, np.exp(np.maximum(diff, 0.0)[None] * log_gamma[:, None, None]), 0.0)
    dmask = dmask.reshape(B_HEADS // 2, 2, B_CHUNK, B_CHUNK).transpose(0, 2, 1, 3).reshape(B_HEADS // 2, B_CHUNK, -1)
    q_decay = np.exp((pos + 1.0)[:, None] * log_gamma[None, :])
    k_decay = np.exp((B_CHUNK - 1.0 - pos)[:, None] * log_gamma[None, :])
    chunk_decay = np.exp(B_CHUNK * log_gamma).reshape(B_HEADS // 2, 2, 1, 1)
    cdec = np.broadcast_to(chunk_decay, (B_HEADS // 2, 2, B_QK_DIM, 2 * B_V_DIM)).reshape(B_HEADS // 2, 2 * B_QK_DIM, -1)
    segm = np.kron(np.eye(2), np.full((B_V_DIM, B_V_DIM), 1.0 / B_V_DIM))
    rep = lambda t: np.repeat(t, B_QK_DIM, axis=1)
    return (jnp.asarray(dmask, F32), jnp.asarray(rep(q_decay), F32), jnp.asarray(rep(k_decay), F32),
            jnp.asarray(cdec, F32), jnp.asarray(segm, BF16))


def _retention(zb, batch, cos, sin, gn):
    n = zb.shape[0]
    nt = n // batch // SEQ_TB
    dmask, qdec, kdec, cdec, segm = _retention_tables()
    consts = [dmask, qdec, kdec, cdec, gn.reshape(1, -1), segm]
    tok = lambda wd: pl.BlockSpec((SEQ_TB, wd), lambda b, t: (b * nt + t, 0))
    return pl.pallas_call(
        _ret_kernel,
        grid=(batch, nt),
        in_specs=[tok(B_IN_WIDTH), tok(LANES), tok(LANES)] + [_const_spec(c.shape) for c in consts],
        out_specs=tok(B_V_WIDTH),
        out_shape=jax.ShapeDtypeStruct((n, B_V_WIDTH), BF16),
        scratch_shapes=[pltpu.VMEM((B_HEADS // 2, 2 * B_QK_DIM, 2 * B_V_DIM), F32)],
        compiler_params=_params("parallel", "arbitrary"),
        name="retention",
    )(zb, cos, sin, *consts)


def _merge_kernel(h_ref, ya_ref, yb_ref, gp_ref, gb_ref, pa_ref, pb_ref, wo_ref, o_ref):
    d = h_ref.shape[1]
    gates = _sigmoid(gp_ref[...] + gb_ref[...])
    merged = (gates[:, :d] * _mm(ya_ref[...], pa_ref[...]) + gates[:, d:] * _mm(yb_ref[...], pb_ref[...]))
    o_ref[...] = h_ref[...] + _mm(merged, wo_ref[...])


def _merge(h, ya, yb, gp, gate_b, p_a, p_b, w_o):
    n, d = h.shape
    tok = lambda wd: pl.BlockSpec((MERGE_TM, wd), lambda i: (i, 0))
    return pl.pallas_call(
        _merge_kernel,
        grid=(n // MERGE_TM,),
        in_specs=[tok(d), tok(A_WIDTH), tok(B_V_WIDTH), tok(2 * d), _const_spec((1, 2 * d)),
                  _const_spec(p_a.shape), _const_spec(p_b.shape), _const_spec(w_o.shape)],
        out_specs=tok(d),
        out_shape=jax.ShapeDtypeStruct((n, d), F32),
        compiler_params=_params("parallel"),
        name="merge",
    )(h, ya, yb, gp, gate_b.reshape(1, 2 * d), p_a, p_b, w_o)


def _memkv_kernel(m_ref, g_ref, w_ref, o_ref):
    o_ref[...] = _mm(_rms(m_ref[...], g_ref[...]), w_ref[...]).astype(BF16)


def _memkv(mem, g, w_kv):
    n, d = mem.shape
    tm = 512
    return pl.pallas_call(
        _memkv_kernel,
        grid=(n // tm,),
        in_specs=[pl.BlockSpec((tm, d), lambda i: (i, 0)), _const_spec((1, d)), _const_spec(w_kv.shape)],
        out_specs=pl.BlockSpec((tm, 2 * d), lambda i: (i, 0)),
        out_shape=jax.ShapeDtypeStruct((n, 2 * d), BF16),
        compiler_params=_params("parallel"),
        name="memkv",
    )(mem, g.reshape(1, d), w_kv)


def _cross_kernel(h_ref, g_ref, wq_ref, kv_ref, wo_ref, o_ref):
    h = h_ref[...]
    d = h.shape[1]
    hd = d // X_HEADS
    q = _mm(_rms(h, g_ref[...]), wq_ref[...])
    heads = range(X_HEADS)
    s = [_mm(q[:, i * hd:(i + 1) * hd], kv_ref[:, i * hd:(i + 1) * hd], _NT) * (hd ** -0.5) for i in heads]
    e = [jnp.exp(s[i] - jnp.max(s[i], axis=-1, keepdims=True)) for i in heads]
    p = [e[i] * (1.0 / jnp.sum(e[i], axis=-1, keepdims=True)) for i in heads]
    outs = [_mm(p[i], kv_ref[:, d + i * hd:d + (i + 1) * hd]) for i in heads]
    o_ref[...] = h + _mm(jnp.concatenate(outs, axis=1), wo_ref[...])


def _cross(h, batch, g, w_q, kv, w_o):
    n, d = h.shape
    nt = n // batch // CROSS_TM
    mlen = kv.shape[0] // batch
    tok = pl.BlockSpec((CROSS_TM, d), lambda b, t: (b * nt + t, 0))
    return pl.pallas_call(
        _cross_kernel,
        grid=(batch, nt),
        in_specs=[tok, _const_spec((1, d)), _const_spec(w_q.shape),
                  pl.BlockSpec((mlen, 2 * d), lambda b, t: (b, 0)), _const_spec(w_o.shape)],
        out_specs=tok,
        out_shape=jax.ShapeDtypeStruct((n, d), F32),
        compiler_params=_params("parallel", "arbitrary"),
        name="cross",
    )(h, g.reshape(1, d), w_q, kv, w_o)


def kernel(x, mem, positions, norm_g, ffn_w_in, ffn_w_out, mix_w_in, mix_gate_b, shift_mu, a_w0, a_w_up, a_a0, a_a_up, a_g_up, a_k_k, a_k_a, a_r_k, a_ln, b_gn, w_branch_a, w_branch_b, mix_w_out, mem_norm, cross_w_q, cross_w_kv, cross_w_o, final_norm):
    batch, seq, d = x.shape
    depth = norm_g.shape[0]
    n = batch * seq
    assert seq % SEQ_TB == 0 and seq % CROSS_TM == 0 and n % ROPE_TM == 0
    assert SEQ_TB % B_CHUNK == 0 and SEQ_TB % RWKV_CHUNK == 0 and 2 ** RWKV_SOLVE_LEVELS == RWKV_CHUNK
    bf = lambda t: t.astype(BF16)

    half = B_QK_DIM // 2
    inv_freq = ROPE_BASE ** (-jnp.arange(half, dtype=F32) / half)
    invf = jnp.tile(inv_freq, LANES // half).reshape(1, LANES)
    lane = np.arange(LANES)
    sign = jnp.asarray(np.where((lane % B_QK_DIM) < half, -1.0, 1.0), F32).reshape(1, LANES)
    cos, sin = _rope_tables(positions.reshape(n, 1), invf, sign)

    seg = jnp.asarray(np.kron(np.eye(LANES // A_HEAD_DIM), np.ones((A_HEAD_DIM, A_HEAD_DIM))), BF16)
    seg2 = jnp.asarray(np.kron(np.eye(2 * LANES // A_HEAD_DIM), np.ones((A_HEAD_DIM, A_HEAD_DIM))), BF16)
    tri = jnp.asarray(np.kron(np.eye(SEQ_TB // RWKV_CHUNK), np.tril(np.ones((RWKV_CHUNK, RWKV_CHUNK)))), BF16)
    mem2 = mem.reshape(-1, d)

    h = x.reshape(n, d)
    for l in range(depth):
        h = _ffn(h, norm_g[l, 0], bf(ffn_w_in[l, 0]), bf(ffn_w_out[l, 0]), final_norm, False)
        za, zb, gp = _inproj(h, norm_g[l, 1], bf(mix_w_in[l]))
        ya = _rwkv(za, batch, shift_mu[l], a_w0[l], bf(a_w_up[l]), a_a0[l], bf(a_a_up[l]), bf(a_g_up[l]),
                   a_k_k[l], a_k_a[l], a_r_k[l], a_ln[l], seg, seg2, tri)
        yb = _retention(zb, batch, cos, sin, b_gn[l])
        h = _merge(h, ya, yb, gp, mix_gate_b[l], bf(w_branch_a[l]), bf(w_branch_b[l]), bf(mix_w_out[l]))
        kv = _memkv(mem2, mem_norm[l], bf(cross_w_kv[l]))
        h = _cross(h, batch, norm_g[l, 2], bf(cross_w_q[l]), kv, bf(cross_w_o[l]))
        h = _ffn(h, norm_g[l, 3], bf(ffn_w_in[l, 1]), bf(ffn_w_out[l, 1]), final_norm, l == depth - 1)
    return h.reshape(batch, seq, d)
```

```python
import functools
import math

import jax
import jax.numpy as jnp
import numpy as np
from jax import lax
from jax.experimental import pallas as pl
from jax.experimental.pallas import tpu as pltpu

F32 = jnp.float32
BF16 = jnp.bfloat16

NORM_EPS = 1e-6
D_FF = 2816
A_HEADS = 8
A_HEAD_DIM = 64
A_WIDTH = A_HEADS * A_HEAD_DIM
A_RANK_W = 64
A_RANK_A = 64
A_RANK_G = 128
A_LN_EPS = 64e-5
A_SHIFT_WIDTH = 3 * A_WIDTH + A_RANK_W + A_RANK_A + A_RANK_G
B_HEADS = 8
B_QK_DIM = 64
B_V_DIM = 128
B_QK_WIDTH = B_HEADS * B_QK_DIM
B_V_WIDTH = B_HEADS * B_V_DIM
B_IN_WIDTH = 2 * B_QK_WIDTH + 2 * B_V_WIDTH
B_CHUNK = 128
B_GN_EPS = 1e-5
ROPE_BASE = 10000.0
X_HEADS = 4

RWKV_CHUNK = 64
RWKV_SOLVE_LEVELS = 6
LANES = 128
VMEM_LIMIT = 56 * 1024 * 1024

FFN_TM = 1024
FFN_TF = 256
INPROJ_TM = 512
SEQ_TB = 512
MERGE_TM = 512
CROSS_TM = 512
ROPE_TM = 1024


def _const_spec(shape):
    nd = len(shape)
    return pl.BlockSpec(shape, lambda *_: (0,) * nd, pipeline_mode=pl.Buffered(1))


def _params(*sem):
    return pltpu.CompilerParams(dimension_semantics=sem, vmem_limit_bytes=VMEM_LIMIT)


def _pieces(x, n):
    if x.dtype == BF16:
        return [x]
    out, rem = [], x
    for i in range(n):
        p = rem.astype(BF16)
        out.append(p)
        if i + 1 < n:
            rem = rem - p.astype(F32)
    return out


_NN = (((1,), (0,)), ((), ()))
_NT = (((1,), (1,)), ((), ()))
_TN = (((0,), (0,)), ((), ()))


def _mm(a, b, dims=_NN, pa=1, pb=1):
    acc = None
    for x in _pieces(a, pa):
        for y in _pieces(b, pb):
            t = lax.dot_general(x, y, dims, preferred_element_type=F32)
            acc = t if acc is None else acc + t
    return acc


def _rms(x, g):
    ms = jnp.mean(x * x, axis=-1, keepdims=True)
    return x * lax.rsqrt(ms + NORM_EPS) * g


def _sigmoid(x):
    return 0.5 * jnp.tanh(0.5 * x) + 0.5


def _ffn_kernel(h_ref, g_ref, wg_ref, wu_ref, wo_ref, fg_ref, o_ref, act_ref, *, final):
    h = h_ref[...]
    u = _rms(h, g_ref[...]).astype(BF16)
    for c in range(D_FF // FFN_TF):
        sl = pl.ds(c * FFN_TF, FFN_TF)
        gate = jnp.dot(u, wg_ref[:, sl], preferred_element_type=F32)
        up = jnp.dot(u, wu_ref[:, sl], preferred_element_type=F32)
        act_ref[:, sl] = (gate * _sigmoid(gate) * up).astype(BF16)
    y = jnp.dot(act_ref[...], wo_ref[...], preferred_element_type=F32)
    out = h + 0.5 * y
    if final:
        out = _rms(out, fg_ref[...])
    o_ref[...] = out


def _ffn(h, g, w_in, w_out, final_g, final):
    n, d = h.shape
    return pl.pallas_call(
        functools.partial(_ffn_kernel, final=final),
        grid=(n // FFN_TM,),
        in_specs=[
            pl.BlockSpec((FFN_TM, d), lambda i: (i, 0)),
            _const_spec((1, d)),
            pl.BlockSpec((d, D_FF), lambda i: (0, 0), pipeline_mode=pl.Buffered(1)),
            pl.BlockSpec((d, D_FF), lambda i: (0, 1), pipeline_mode=pl.Buffered(1)),
            _const_spec((D_FF, d)),
            _const_spec((1, d)),
        ],
        out_specs=pl.BlockSpec((FFN_TM, d), lambda i: (i, 0)),
        out_shape=jax.ShapeDtypeStruct((n, d), F32),
        scratch_shapes=[pltpu.VMEM((FFN_TM, D_FF), BF16)],
        compiler_params=_params("parallel"),
        name="ffn",
    )(h, g.reshape(1, d), w_in, w_in, w_out, final_g.reshape(1, d))


def _inproj_kernel(h_ref, g_ref, w_ref, za_ref, zb_ref, gp_ref):
    u = _rms(h_ref[...], g_ref[...]).astype(BF16)
    c0, c1 = A_SHIFT_WIDTH, A_SHIFT_WIDTH + B_IN_WIDTH
    za_ref[...] = jnp.dot(u, w_ref[:, :c0], preferred_element_type=F32)
    zb_ref[...] = jnp.dot(u, w_ref[:, c0:c1], preferred_element_type=F32)
    gp_ref[...] = jnp.dot(u, w_ref[:, c1:], preferred_element_type=F32).astype(gp_ref.dtype)


def _inproj(h, g, w):
    n, d = h.shape
    widths = (A_SHIFT_WIDTH, B_IN_WIDTH, w.shape[1] - A_SHIFT_WIDTH - B_IN_WIDTH)
    return pl.pallas_call(
        _inproj_kernel,
        grid=(n // INPROJ_TM,),
        in_specs=[
            pl.BlockSpec((INPROJ_TM, d), lambda i: (i, 0)),
            _const_spec((1, d)),
            _const_spec(w.shape),
        ],
        out_specs=[pl.BlockSpec((INPROJ_TM, wd), lambda i: (i, 0)) for wd in widths],
        out_shape=[jax.ShapeDtypeStruct((n, wd), dt) for wd, dt in zip(widths, (F32, F32, BF16))],
        compiler_params=_params("parallel"),
        name="inproj",
    )(h, g.reshape(1, d), w)


def _seg_sum(x, seg_ref):
    wd = seg_ref.shape[0]
    return jnp.concatenate([_mm(x[:, i:i + wd], seg_ref[...]) for i in range(0, x.shape[1], wd)], axis=1)


def _seg_sum_two(x1, x2, seg2_ref):
    wd = seg2_ref.shape[0] // 2
    o1, o2 = [], []
    for i in range(0, x1.shape[1], wd):
        s = _mm(jnp.concatenate([x1[:, i:i + wd], x2[:, i:i + wd]], axis=1), seg2_ref[...])
        o1.append(s[:, :wd])
        o2.append(s[:, wd:])
    return jnp.concatenate(o1, axis=1), jnp.concatenate(o2, axis=1)


def _rwkv_kernel(za_ref, mu_ref, w0_ref, wup_ref, a0_ref, aup_ref, gup_ref, kk_ref, ka_ref, rk_ref, ln_ref,
                 seg_ref, seg2_ref, tri_ref, y_ref, carry_ref, state_ref):
    L = RWKV_CHUNK
    hd = A_HEAD_DIM
    W2 = 2 * hd
    assert W2 == LANES and L == hd

    @pl.when(pl.program_id(1) == 0)
    def _():
        carry_ref[...] = jnp.zeros_like(carry_ref)
        state_ref[...] = jnp.zeros_like(state_ref)

    z = za_ref[...]
    tb = z.shape[0]
    row = lax.broadcasted_iota(jnp.int32, (tb, 1), 0)
    prev = jnp.where(row == 0, carry_ref[...], pltpu.roll(z, 1, 0))
    carry_ref[...] = z[tb - 1:tb, :]
    xs = z + (prev - z) * mu_ref[...]

    r = xs[:, 0:A_WIDTH]
    k = xs[:, A_WIDTH:2 * A_WIDTH]
    v = xs[:, 2 * A_WIDTH:3 * A_WIDTH]
    o = 3 * A_WIDTH
    w_lo = xs[:, o:o + A_RANK_W]
    a_lo = xs[:, o + A_RANK_W:o + A_RANK_W + A_RANK_A]
    g_lo = xs[:, o + A_RANK_W + A_RANK_A:]

    w = w0_ref[...] + _mm(jnp.tanh(w_lo), wup_ref[...])
    logd = -_sigmoid(w) * math.exp(-0.5)
    a = _sigmoid(a0_ref[...] + _mm(a_lo, aup_ref[...]))
    g = _mm(_sigmoid(g_lo), gup_ref[...])

    kk = k * kk_ref[...]
    kk = kk * lax.rsqrt(jnp.maximum(_seg_sum(kk * kk, seg_ref), 1e-24))
    k = k * (1.0 + (a - 1.0) * ka_ref[...])
    kka = kk * a

    cum = _mm(tri_ref[...], logd, pb=2)
    tot = jnp.concatenate([jnp.broadcast_to(cum[c + L - 1:c + L, :], (L, A_WIDTH)) for c in range(0, tb, L)], axis=0)
    e_inc = jnp.exp(cum)
    e_exc = jnp.exp(cum - logd)
    e_neg = jnp.exp(-cum)
    e_end = jnp.exp(tot - cum)
    p_end = jnp.exp(tot)
    rt = r * e_inc
    at = -kk * e_exc
    bt = kka * e_neg
    kt = k * e_neg
    bh = kka * e_end
    kh = k * e_end

    def iota(shape, dim):
        return lax.broadcasted_iota(jnp.int32, shape, dim)

    first1 = (iota((L, W2), 1) & (W2 - 1)) < hd
    first2 = (iota((L, 2 * W2), 1) & (W2 - 1)) < hd
    strict_lower = (iota((L, 2 * W2), 1) & (L - 1)) < iota((L, 2 * W2), 0)
    lower = (iota((L, 2 * W2), 1) & (L - 1)) <= iota((L, 2 * W2), 0)
    eye_pair = jnp.where((iota((L, W2), 1) & (L - 1)) == iota((L, W2), 0), 1.0, 0.0)
    eye_full = iota((W2, W2), 0) == iota((W2, W2), 1)
    same_head = (iota((W2, 2 * W2), 0) < hd) == ((iota((W2, 2 * W2), 1) & (W2 - 1)) < hd)
    zeros_l = jnp.zeros((L, W2), BF16)
    zeros_2l = jnp.zeros((2 * L, W2), BF16)

    def stack(x):
        first = first1 if x.shape[1] == W2 else first2
        return jnp.concatenate([jnp.where(first, x, 0.0), jnp.where(first, 0.0, x)], axis=0)

    units = [(c, p) for c in range(tb // L) for p in range(A_WIDTH // W2)]

    def sl(x, u):
        c, p = u
        return x[c * L:(c + 1) * L, p * W2:(p + 1) * W2]

    at_b, rt_b, bt_b, kt_b, bh_b, kh_b, v_b = (x.astype(BF16) for x in (at, rt, bt, kt, bh, kh, v))

    a_ab, a_ak, a_r = {}, {}, {}
    for u in units:
        lhs = jnp.concatenate([sl(at_b, u), sl(rt_b, u)], axis=0)
        rhs = jnp.concatenate([stack(sl(bt_b, u)), stack(sl(kt_b, u))], axis=0)
        m1 = _mm(lhs, rhs, _NT)
        a_a = jnp.where(strict_lower, m1[:L], 0.0)
        a_ab[u] = a_a[:, :W2]
        a_ak[u] = a_a[:, W2:].astype(BF16)
        a_r[u] = jnp.where(lower, m1[L:], 0.0).astype(BF16)
    x1 = {u: _mm(a_ak[u], stack(sl(v_b, u))) for u in units}

    tinv = {u: eye_pair + a_ab[u] for u in units}
    pw = {u: a_ab[u].astype(BF16) for u in units}
    pw = {u: _mm(pw[u], stack(pw[u])).astype(BF16) for u in units}
    for lvl in range(1, RWKV_SOLVE_LEVELS):
        for u in units:
            t_b = tinv[u].astype(BF16)
            if lvl + 1 < RWKV_SOLVE_LEVELS:
                prod = _mm(pw[u], stack(jnp.concatenate([t_b, pw[u]], axis=1)))
                tinv[u] = tinv[u] + prod[:, :W2]
                pw[u] = prod[:, W2:].astype(BF16)
            else:
                tinv[u] = tinv[u] + _mm(pw[u], stack(t_b))
    sol = {u: _mm(tinv[u], stack(jnp.concatenate([sl(at_b, u), x1[u].astype(BF16)], axis=1))).astype(BF16)
           for u in units}

    qp, yp, gbd, hp = {}, {}, {}, {}
    for u in units:
        v_u = sl(v_b, u)
        rhs_a = jnp.concatenate([stack(sol[u]), jnp.concatenate([zeros_2l, stack(v_u)], axis=1)], axis=0)
        m2a = _mm(a_r[u], rhs_a)
        rhs_b = jnp.concatenate([sol[u], jnp.concatenate([zeros_l, v_u], axis=1)], axis=0)
        m2b = _mm(jnp.concatenate([sl(bh_b, u), sl(kh_b, u)], axis=0), rhs_b, _TN)
        m2b = jnp.where(same_head, m2b, 0.0)
        qp[u] = sl(rt, u) + m2a[:, :W2]
        yp[u] = m2a[:, W2:]
        c, p = u
        gbd[u] = jnp.where(eye_full, p_end[c * L:c * L + 1, p * W2:(p + 1) * W2], 0.0) + m2b[:, :W2]
        hp[u] = m2b[:, W2:]

    ys = {}
    for u in units:
        m3 = _mm(jnp.concatenate([qp[u], gbd[u]], axis=0), state_ref[u[1]])
        ys[u] = m3[:L] + yp[u]
        state_ref[u[1]] = m3[L:] + hp[u]

    y = jnp.concatenate([jnp.concatenate([ys[(c, p)] for p in range(A_WIDTH // W2)], axis=1)
                         for c in range(tb // L)], axis=0)
    inv_n = 1.0 / hd
    y_sum, rk_sum = _seg_sum_two(y, r * k * rk_ref[...], seg2_ref)
    yc = y - y_sum * inv_n
    var = _seg_sum(yc * yc, seg_ref) * inv_n
    yn = yc * lax.rsqrt(var + A_LN_EPS) * ln_ref[0:1, :] + ln_ref[1:2, :]
    y_ref[...] = ((yn + rk_sum * v) * g).astype(y_ref.dtype)


def _rwkv(za, batch, mu, w0, w_up, a0, a_up, g_up, k_k, k_a, r_k, ln, seg, seg2, tri):
    n = za.shape[0]
    nt = n // batch // SEQ_TB
    row = lambda p: p.reshape(1, -1)
    consts = [row(mu), row(w0), w_up, row(a0), a_up, g_up, row(k_k), row(k_a), row(r_k), ln, seg, seg2, tri]
    return pl.pallas_call(
        _rwkv_kernel,
        grid=(batch, nt),
        in_specs=[pl.BlockSpec((SEQ_TB, A_SHIFT_WIDTH), lambda b, t: (b * nt + t, 0))]
        + [_const_spec(c.shape) for c in consts],
        out_specs=pl.BlockSpec((SEQ_TB, A_WIDTH), lambda b, t: (b * nt + t, 0)),
        out_shape=jax.ShapeDtypeStruct((n, A_WIDTH), BF16),
        scratch_shapes=[pltpu.VMEM((1, A_SHIFT_WIDTH), F32),
                        pltpu.VMEM((A_HEADS // 2, 2 * A_HEAD_DIM, 2 * A_HEAD_DIM), F32)],
        compiler_params=_params("parallel", "arbitrary"),
        name="rwkv",
    )(za, *consts)


def _rope_kernel(pos_ref, invf_ref, sign_ref, cos_ref, sin_ref):
    ang = pos_ref[...].astype(F32) * invf_ref[...]
    cos_ref[...] = jnp.cos(ang)
    sin_ref[...] = jnp.sin(ang) * sign_ref[...]


def _rope_tables(pos, invf, sign):
    n = pos.shape[0]
    return pl.pallas_call(
        _rope_kernel,
        grid=(n // ROPE_TM,),
        in_specs=[pl.BlockSpec((ROPE_TM, 1), lambda i: (i, 0)), _const_spec((1, LANES)), _const_spec((1, LANES))],
        out_specs=[pl.BlockSpec((ROPE_TM, LANES), lambda i: (i, 0))] * 2,
        out_shape=[jax.ShapeDtypeStruct((n, LANES), F32)] * 2,
        compiler_params=_params("parallel"),
        name="rope",
    )(pos, invf, sign)


def _ret_kernel(zb_ref, cos_ref, sin_ref, dmask_ref, qdec_ref, kdec_ref, cdec_ref, gn_ref, segm_ref, y_ref, state_ref):
    C = B_CHUNK
    dk, dv = B_QK_DIM, B_V_DIM
    QW, VW = 2 * dk, 2 * dv
    assert QW == LANES and dv == LANES

    @pl.when(pl.program_id(1) == 0)
    def _():
        state_ref[...] = jnp.zeros_like(state_ref)

    def iota(shape, dim):
        return lax.broadcasted_iota(jnp.int32, shape, dim)

    first_half = (iota((C, B_QK_WIDTH), 1) & (dk - 1)) < dk // 2
    first_q = iota((C, QW), 1) < dk
    first_v = iota((C, VW), 1) < dv
    same_head = (iota((QW, VW), 0) < dk) == (iota((QW, VW), 1) < dv)

    def rotate(x, cos, sin):
        partner = jnp.where(first_half, pltpu.roll(x, B_QK_WIDTH - dk // 2, 1), pltpu.roll(x, dk // 2, 1))
        return x * cos + partner * sin

    def stack(x, first):
        return jnp.concatenate([jnp.where(first, x, 0.0), jnp.where(first, 0.0, x)], axis=0)

    chunks = range(zb_ref.shape[0] // C)
    q, k, qd, kd = {}, {}, {}, {}
    for c in chunks:
        rs = slice(c * C, (c + 1) * C)
        reps = B_QK_WIDTH // LANES
        cos = jnp.concatenate([cos_ref[rs, :]] * reps, axis=1)
        sin = jnp.concatenate([sin_ref[rs, :]] * reps, axis=1)
        q[c] = rotate(zb_ref[rs, 0:B_QK_WIDTH], cos, sin)
        k[c] = rotate(zb_ref[rs, B_QK_WIDTH:2 * B_QK_WIDTH], cos, sin) * (dk ** -0.5)
        qd[c] = q[c] * qdec_ref[...]
        kd[c] = k[c] * kdec_ref[...]

    units = [(c, p) for c in chunks for p in range(B_HEADS // 2)]

    def qk(x, u):
        return x[u[0]][:, u[1] * QW:(u[1] + 1) * QW]

    def vg(base, u):
        c, p = u
        return zb_ref[c * C:(c + 1) * C, base + p * VW:base + (p + 1) * VW]

    v = {u: vg(2 * B_QK_WIDTH, u) for u in units}
    scores = {u: _mm(qk(q, u), stack(qk(k, u), first_q), _NT) * dmask_ref[u[1]] for u in units}
    inner = {u: _mm(scores[u], stack(v[u], first_v)) for u in units}
    kv = {u: jnp.where(same_head, _mm(qk(kd, u), v[u], _TN), 0.0) for u in units}

    before = {}
    for p in range(B_HEADS // 2):
        st = state_ref[p]
        for c in chunks:
            before[(c, p)] = st
            st = st * cdec_ref[p] + kv[(c, p)]
        state_ref[p] = st

    y = {u: inner[u] + _mm(qk(qd, u), before[u]) for u in units}
    yc = {u: y[u] - _mm(y[u], segm_ref[...]) for u in units}
    var = {u: _mm(yc[u] * yc[u], segm_ref[...]) for u in units}
    for u in units:
        c, p = u
        g = vg(2 * B_QK_WIDTH + B_V_WIDTH, u)
        out = yc[u] * lax.rsqrt(var[u] + B_GN_EPS) * gn_ref[:, p * VW:(p + 1) * VW] * (g * _sigmoid(g))
        y_ref[c * C:(c + 1) * C, p * VW:(p + 1) * VW] = out.astype(y_ref.dtype)


def _retention_tables():
    heads = np.arange(B_HEADS, dtype=np.float64)
    log_gamma = np.log1p(-np.exp2(-5.0 - heads))
    pos = np.arange(B_CHUNK, dtype=np.float64)
    diff = pos[:, None] - pos[None, :]
    dmask = np.where(diff[None] >= 0, np.exp(np.maximum(diff, 0.0)[None] * log_gamma[:, None, None]), 0.0)
    dmask = dmask.reshape(B_HEADS // 2, 2, B_CHUNK, B_CHUNK).transpose(0, 2, 1, 3).reshape(B_HEADS // 2, B_CHUNK, -1)
    q_decay = np.exp((pos + 1.0)[:, None] * log_gamma[None, :])
    k_decay = np.exp((B_CHUNK - 1.0 - pos)[:, None] * log_gamma[None, :])
    chunk_decay = np.exp(B_CHUNK * log_gamma).reshape(B_HEADS // 2, 2, 1, 1)
    cdec = np.broadcast_to(chunk_decay, (B_HEADS // 2, 2, B_QK_DIM, 2 * B_V_DIM)).reshape(B_HEADS // 2, 2 * B_QK_DIM, -1)
    segm = np.kron(np.eye(2), np.full((B_V_DIM, B_V_DIM), 1.0 / B_V_DIM))
    rep = lambda t: np.repeat(t, B_QK_DIM, axis=1)
    return (jnp.asarray(dmask, F32), jnp.asarray(rep(q_decay), F32), jnp.asarray(rep(k_decay), F32),
            jnp.asarray(cdec, F32), jnp.asarray(segm, BF16))


def _retention(zb, batch, cos, sin, gn):
    n = zb.shape[0]
    nt = n // batch // SEQ_TB
    dmask, qdec, kdec, cdec, segm = _retention_tables()
    consts = [dmask, qdec, kdec, cdec, gn.reshape(1, -1), segm]
    tok = lambda wd: pl.BlockSpec((SEQ_TB, wd), lambda b, t: (b * nt + t, 0))
    return pl.pallas_call(
        _ret_kernel,
        grid=(batch, nt),
        in_specs=[tok(B_IN_WIDTH), tok(LANES), tok(LANES)] + [_const_spec(c.shape) for c in consts],
        out_specs=tok(B_V_WIDTH),
        out_shape=jax.ShapeDtypeStruct((n, B_V_WIDTH), BF16),
        scratch_shapes=[pltpu.VMEM((B_HEADS // 2, 2 * B_QK_DIM, 2 * B_V_DIM), F32)],
        compiler_params=_params("parallel", "arbitrary"),
        name="retention",
    )(zb, cos, sin, *consts)


def _merge_kernel(h_ref, ya_ref, yb_ref, gp_ref, gb_ref, pa_ref, pb_ref, wo_ref, o_ref):
    d = h_ref.shape[1]
    gates = _sigmoid(gp_ref[...] + gb_ref[...])
    merged = (gates[:, :d] * _mm(ya_ref[...], pa_ref[...]) + gates[:, d:] * _mm(yb_ref[...], pb_ref[...]))
    o_ref[...] = h_ref[...] + _mm(merged, wo_ref[...])


def _merge(h, ya, yb, gp, gate_b, p_a, p_b, w_o):
    n, d = h.shape
    tok = lambda wd: pl.BlockSpec((MERGE_TM, wd), lambda i: (i, 0))
    return pl.pallas_call(
        _merge_kernel,
        grid=(n // MERGE_TM,),
        in_specs=[tok(d), tok(A_WIDTH), tok(B_V_WIDTH), tok(2 * d), _const_spec((1, 2 * d)),
                  _const_spec(p_a.shape), _const_spec(p_b.shape), _const_spec(w_o.shape)],
        out_specs=tok(d),
        out_shape=jax.ShapeDtypeStruct((n, d), F32),
        compiler_params=_params("parallel"),
        name="merge",
    )(h, ya, yb, gp, gate_b.reshape(1, 2 * d), p_a, p_b, w_o)


def _memkv_kernel(m_ref, g_ref, w_ref, o_ref):
    o_ref[...] = _mm(_rms(m_ref[...], g_ref[...]), w_ref[...]).astype(BF16)


def _memkv(mem, g, w_kv):
    n, d = mem.shape
    tm = 512
    return pl.pallas_call(
        _memkv_kernel,
        grid=(n // tm,),
        in_specs=[pl.BlockSpec((tm, d), lambda i: (i, 0)), _const_spec((1, d)), _const_spec(w_kv.shape)],
        out_specs=pl.BlockSpec((tm, 2 * d), lambda i: (i, 0)),
        out_shape=jax.ShapeDtypeStruct((n, 2 * d), BF16),
        compiler_params=_params("parallel"),
        name="memkv",
    )(mem, g.reshape(1, d), w_kv)


def _cross_kernel(h_ref, g_ref, wq_ref, kv_ref, wo_ref, o_ref):
    h = h_ref[...]
    d = h.shape[1]
    hd = d // X_HEADS
    q = _mm(_rms(h, g_ref[...]), wq_ref[...])
    heads = range(X_HEADS)
    s = [_mm(q[:, i * hd:(i + 1) * hd], kv_ref[:, i * hd:(i + 1) * hd], _NT) * (hd ** -0.5) for i in heads]
    e = [jnp.exp(s[i] - jnp.max(s[i], axis=-1, keepdims=True)) for i in heads]
    p = [e[i] * (1.0 / jnp.sum(e[i], axis=-1, keepdims=True)) for i in heads]
    outs = [_mm(p[i], kv_ref[:, d + i * hd:d + (i + 1) * hd]) for i in heads]
    o_ref[...] = h + _mm(jnp.concatenate(outs, axis=1), wo_ref[...])


def _cross(h, batch, g, w_q, kv, w_o):
    n, d = h.shape
    nt = n // batch // CROSS_TM
    mlen = kv.shape[0] // batch
    tok = pl.BlockSpec((CROSS_TM, d), lambda b, t: (b * nt + t, 0))
    return pl.pallas_call(
        _cross_kernel,
        grid=(batch, nt),
        in_specs=[tok, _const_spec((1, d)), _const_spec(w_q.shape),
                  pl.BlockSpec((mlen, 2 * d), lambda b, t: (b, 0)), _const_spec(w_o.shape)],
        out_specs=tok,
        out_shape=jax.ShapeDtypeStruct((n, d), F32),
        compiler_params=_params("parallel", "arbitrary"),
        name="cross",
    )(h, g.reshape(1, d), w_q, kv, w_o)


def kernel(x, mem, positions, norm_g, ffn_w_in, ffn_w_out, mix_w_in, mix_gate_b, shift_mu, a_w0, a_w_up, a_a0, a_a_up, a_g_up, a_k_k, a_k_a, a_r_k, a_ln, b_gn, w_branch_a, w_branch_b, mix_w_out, mem_norm, cross_w_q, cross_w_kv, cross_w_o, final_norm):
    batch, seq, d = x.shape
    depth = norm_g.shape[0]
    n = batch * seq
    assert seq % SEQ_TB == 0 and seq % CROSS_TM == 0 and n % ROPE_TM == 0
    assert n % FFN_TM == 0 and n % INPROJ_TM == 0 and n % MERGE_TM == 0
    assert SEQ_TB % B_CHUNK == 0 and SEQ_TB % RWKV_CHUNK == 0 and 2 ** RWKV_SOLVE_LEVELS == RWKV_CHUNK
    bf = lambda t: t.astype(BF16)

    half = B_QK_DIM // 2
    inv_freq = ROPE_BASE ** (-jnp.arange(half, dtype=F32) / half)
    invf = jnp.tile(inv_freq, LANES // half).reshape(1, LANES)
    lane = np.arange(LANES)
    sign = jnp.asarray(np.where((lane % B_QK_DIM) < half, -1.0, 1.0), F32).reshape(1, LANES)
    cos, sin = _rope_tables(positions.reshape(n, 1), invf, sign)

    seg = jnp.asarray(np.kron(np.eye(LANES // A_HEAD_DIM), np.ones((A_HEAD_DIM, A_HEAD_DIM))), BF16)
    seg2 = jnp.asarray(np.kron(np.eye(2 * LANES // A_HEAD_DIM), np.ones((A_HEAD_DIM, A_HEAD_DIM))), BF16)
    tri = jnp.asarray(np.kron(np.eye(SEQ_TB // RWKV_CHUNK), np.tril(np.ones((RWKV_CHUNK, RWKV_CHUNK)))), BF16)
    mem2 = mem.reshape(-1, d)

    h = x.reshape(n, d)
    for l in range(depth):
        h = _ffn(h, norm_g[l, 0], bf(ffn_w_in[l, 0]), bf(ffn_w_out[l, 0]), final_norm, False)
        za, zb, gp = _inproj(h, norm_g[l, 1], bf(mix_w_in[l]))
        ya = _rwkv(za, batch, shift_mu[l], a_w0[l], bf(a_w_up[l]), a_a0[l], bf(a_a_up[l]), bf(a_g_up[l]),
                   a_k_k[l], a_k_a[l], a_r_k[l], a_ln[l], seg, seg2, tri)
        yb = _retention(zb, batch, cos, sin, b_gn[l])
        h = _merge(h, ya, yb, gp, mix_gate_b[l], bf(w_branch_a[l]), bf(w_branch_b[l]), bf(mix_w_out[l]))
        kv = _memkv(mem2, mem_norm[l], bf(cross_w_kv[l]))
        h = _cross(h, batch, norm_g[l, 2], bf(cross_w_q[l]), kv, bf(cross_w_o[l]))
        h = _ffn(h, norm_g[l, 3], bf(ffn_w_in[l, 1]), bf(ffn_w_out[l, 1]), final_norm, l == depth - 1)
    return h.reshape(batch, seq, d)
```

```python
import functools
import math

import jax
import jax.numpy as jnp
import numpy as np
from jax import lax
from jax.experimental import pallas as pl
from jax.experimental.pallas import tpu as pltpu

F32 = jnp.float32
BF16 = jnp.bfloat16

NORM_EPS = 1e-6
D_FF = 2816
A_HEADS = 8
A_HEAD_DIM = 64
A_WIDTH = A_HEADS * A_HEAD_DIM
A_RANK_W = 64
A_RANK_A = 64
A_RANK_G = 128
A_LN_EPS = 64e-5
A_SHIFT_WIDTH = 3 * A_WIDTH + A_RANK_W + A_RANK_A + A_RANK_G
B_HEADS = 8
B_QK_DIM = 64
B_V_DIM = 128
B_QK_WIDTH = B_HEADS * B_QK_DIM
B_V_WIDTH = B_HEADS * B_V_DIM
B_IN_WIDTH = 2 * B_QK_WIDTH + 2 * B_V_WIDTH
B_CHUNK = 128
B_GN_EPS = 1e-5
ROPE_BASE = 10000.0
X_HEADS = 4

RWKV_CHUNK = 64
RWKV_SOLVE_LEVELS = 6
LANES = 128
VMEM_LIMIT = 56 * 1024 * 1024

FFN_TM = 1024
FFN_TF = 256
INPROJ_TM = 512
SEQ_TB = 512
MERGE_TM = 1024
CROSS_TM = 1024
ROPE_TM = 1024
TRI_ROWS = 256


def _const_spec(shape):
    nd = len(shape)
    return pl.BlockSpec(shape, lambda *_: (0,) * nd, pipeline_mode=pl.Buffered(1))


def _params(*sem):
    return pltpu.CompilerParams(dimension_semantics=sem, vmem_limit_bytes=VMEM_LIMIT)


def _pieces(x, n):
    if x.dtype == BF16:
        return [x]
    out, rem = [], x
    for i in range(n):
        p = rem.astype(BF16)
        out.append(p)
        if i + 1 < n:
            rem = rem - p.astype(F32)
    return out


_NN = (((1,), (0,)), ((), ()))
_NT = (((1,), (1,)), ((), ()))
_TN = (((0,), (0,)), ((), ()))


def _mm(a, b, dims=_NN, pa=1, pb=1):
    acc = None
    for x in _pieces(a, pa):
        for y in _pieces(b, pb):
            t = lax.dot_general(x, y, dims, preferred_element_type=F32)
            acc = t if acc is None else acc + t
    return acc


def _rms(x, g):
    ms = jnp.mean(x * x, axis=-1, keepdims=True)
    return x * lax.rsqrt(ms + NORM_EPS) * g


def _sigmoid(x):
    return 0.5 * jnp.tanh(0.5 * x) + 0.5


def _ffn_kernel(h_ref, g_ref, wg_ref, wu_ref, wo_ref, fg_ref, o_ref, act_ref, *, final):
    h = h_ref[...]
    u = _rms(h, g_ref[...]).astype(BF16)
    for c in range(D_FF // FFN_TF):
        sl = pl.ds(c * FFN_TF, FFN_TF)
        gate = jnp.dot(u, wg_ref[:, sl], preferred_element_type=F32)
        up = jnp.dot(u, wu_ref[:, sl], preferred_element_type=F32)
        act_ref[:, sl] = (gate * _sigmoid(gate) * up).astype(BF16)
    y = jnp.dot(act_ref[...], wo_ref[...], preferred_element_type=F32)
    out = h + 0.5 * y
    if final:
        out = _rms(out, fg_ref[...])
    o_ref[...] = out


def _ffn(h, g, w_in, w_out, final_g, final):
    n, d = h.shape
    return pl.pallas_call(
        functools.partial(_ffn_kernel, final=final),
        grid=(n // FFN_TM,),
        in_specs=[
            pl.BlockSpec((FFN_TM, d), lambda i: (i, 0)),
            _const_spec((1, d)),
            pl.BlockSpec((d, D_FF), lambda i: (0, 0), pipeline_mode=pl.Buffered(1)),
            pl.BlockSpec((d, D_FF), lambda i: (0, 1), pipeline_mode=pl.Buffered(1)),
            _const_spec((D_FF, d)),
            _const_spec((1, d)),
        ],
        out_specs=pl.BlockSpec((FFN_TM, d), lambda i: (i, 0)),
        out_shape=jax.ShapeDtypeStruct((n, d), F32),
        scratch_shapes=[pltpu.VMEM((FFN_TM, D_FF), BF16)],
        compiler_params=_params("parallel"),
        name="ffn",
    )(h, g.reshape(1, d), w_in, w_in, w_out, final_g.reshape(1, d))


def _inproj_kernel(h_ref, g_ref, w_ref, za_ref, zb_ref, gp_ref):
    u = _rms(h_ref[...], g_ref[...]).astype(BF16)
    c0, c1 = A_SHIFT_WIDTH, A_SHIFT_WIDTH + B_IN_WIDTH
    za_ref[...] = jnp.dot(u, w_ref[:, :c0], preferred_element_type=F32)
    zb_ref[...] = jnp.dot(u, w_ref[:, c0:c1], preferred_element_type=F32)
    gp_ref[...] = jnp.dot(u, w_ref[:, c1:], preferred_element_type=F32).astype(gp_ref.dtype)


def _inproj(h, g, w):
    n, d = h.shape
    widths = (A_SHIFT_WIDTH, B_IN_WIDTH, w.shape[1] - A_SHIFT_WIDTH - B_IN_WIDTH)
    return pl.pallas_call(
        _inproj_kernel,
        grid=(n // INPROJ_TM,),
        in_specs=[
            pl.BlockSpec((INPROJ_TM, d), lambda i: (i, 0)),
            _const_spec((1, d)),
            _const_spec(w.shape),
        ],
        out_specs=[pl.BlockSpec((INPROJ_TM, wd), lambda i: (i, 0)) for wd in widths],
        out_shape=[jax.ShapeDtypeStruct((n, wd), dt) for wd, dt in zip(widths, (F32, F32, BF16))],
        compiler_params=_params("parallel"),
        name="inproj",
    )(h, g.reshape(1, d), w)


def _seg_sum(x, seg_ref):
    wd = seg_ref.shape[0]
    return jnp.concatenate([_mm(x[:, i:i + wd], seg_ref[...]) for i in range(0, x.shape[1], wd)], axis=1)


def _seg_sum_two(x1, x2, seg2_ref):
    wd = seg2_ref.shape[0] // 2
    o1, o2 = [], []
    for i in range(0, x1.shape[1], wd):
        s = _mm(jnp.concatenate([x1[:, i:i + wd], x2[:, i:i + wd]], axis=1), seg2_ref[...])
        o1.append(s[:, :wd])
        o2.append(s[:, wd:])
    return jnp.concatenate(o1, axis=1), jnp.concatenate(o2, axis=1)


def _rwkv_kernel(za_ref, mu_ref, w0_ref, wup_ref, a0_ref, aup_ref, gup_ref, kk_ref, ka_ref, rk_ref, ln_ref,
                 seg_ref, seg2_ref, tri_ref, y_ref, carry_ref, state_ref):
    L = RWKV_CHUNK
    hd = A_HEAD_DIM
    W2 = 2 * hd
    assert W2 == LANES and L == hd

    @pl.when(pl.program_id(1) == 0)
    def _():
        carry_ref[...] = jnp.zeros_like(carry_ref)
        state_ref[...] = jnp.zeros_like(state_ref)

    z = za_ref[...]
    tb = z.shape[0]
    row = lax.broadcasted_iota(jnp.int32, (tb, 1), 0)
    prev = jnp.where(row == 0, carry_ref[...], pltpu.roll(z, 1, 0))
    carry_ref[...] = z[tb - 1:tb, :]
    xs = z + (prev - z) * mu_ref[...]

    r = xs[:, 0:A_WIDTH]
    k = xs[:, A_WIDTH:2 * A_WIDTH]
    v = xs[:, 2 * A_WIDTH:3 * A_WIDTH]
    o = 3 * A_WIDTH
    w_lo = xs[:, o:o + A_RANK_W]
    a_lo = xs[:, o + A_RANK_W:o + A_RANK_W + A_RANK_A]
    g_lo = xs[:, o + A_RANK_W + A_RANK_A:]

    w = w0_ref[...] + _mm(jnp.tanh(w_lo), wup_ref[...])
    logd = -_sigmoid(w) * math.exp(-0.5)
    a = _sigmoid(a0_ref[...] + _mm(a_lo, aup_ref[...]))
    g = _mm(_sigmoid(g_lo), gup_ref[...])

    kk = k * kk_ref[...]
    kk = kk * lax.rsqrt(jnp.maximum(_seg_sum(kk * kk, seg_ref), 1e-24))
    k = k * (1.0 + (a - 1.0) * ka_ref[...])
    kka = kk * a

    tr = tri_ref.shape[0]
    cum = jnp.concatenate([_mm(tri_ref[...], logd[i:i + tr], pb=2) for i in range(0, tb, tr)], axis=0)
    tot = jnp.concatenate([jnp.broadcast_to(cum[c + L - 1:c + L, :], (L, A_WIDTH)) for c in range(0, tb, L)], axis=0)
    e_inc = jnp.exp(cum)
    e_exc = jnp.exp(cum - logd)
    e_neg = jnp.exp(-cum)
    e_end = jnp.exp(tot - cum)
    p_end = jnp.exp(tot)
    rt = r * e_inc
    at = -kk * e_exc
    bt = kka * e_neg
    kt = k * e_neg
    bh = kka * e_end
    kh = k * e_end

    def iota(shape, dim):
        return lax.broadcasted_iota(jnp.int32, shape, dim)

    first1 = (iota((L, W2), 1) & (W2 - 1)) < hd
    first2 = (iota((L, 2 * W2), 1) & (W2 - 1)) < hd
    strict_lower = (iota((L, 2 * W2), 1) & (L - 1)) < iota((L, 2 * W2), 0)
    lower = (iota((L, 2 * W2), 1) & (L - 1)) <= iota((L, 2 * W2), 0)
    eye_pair = jnp.where((iota((L, W2), 1) & (L - 1)) == iota((L, W2), 0), 1.0, 0.0)
    eye_full = iota((W2, W2), 0) == iota((W2, W2), 1)
    same_head = (iota((W2, 2 * W2), 0) < hd) == ((iota((W2, 2 * W2), 1) & (W2 - 1)) < hd)
    zeros_l = jnp.zeros((L, W2), BF16)
    zeros_2l = jnp.zeros((2 * L, W2), BF16)

    def stack(x):
        first = first1 if x.shape[1] == W2 else first2
        return jnp.concatenate([jnp.where(first, x, 0.0), jnp.where(first, 0.0, x)], axis=0)

    units = [(c, p) for c in range(tb // L) for p in range(A_WIDTH // W2)]

    def sl(x, u):
        c, p = u
        return x[c * L:(c + 1) * L, p * W2:(p + 1) * W2]

    at_b, rt_b, bt_b, kt_b, bh_b, kh_b, v_b = (x.astype(BF16) for x in (at, rt, bt, kt, bh, kh, v))

    a_ab, a_ak, a_r = {}, {}, {}
    for u in units:
        lhs = jnp.concatenate([sl(at_b, u), sl(rt_b, u)], axis=0)
        rhs = jnp.concatenate([stack(sl(bt_b, u)), stack(sl(kt_b, u))], axis=0)
        m1 = _mm(lhs, rhs, _NT)
        a_a = jnp.where(strict_lower, m1[:L], 0.0)
        a_ab[u] = a_a[:, :W2]
        a_ak[u] = a_a[:, W2:].astype(BF16)
        a_r[u] = jnp.where(lower, m1[L:], 0.0).astype(BF16)
    x1 = {u: _mm(a_ak[u], stack(sl(v_b, u))) for u in units}

    tinv = {u: eye_pair + a_ab[u] for u in units}
    pw = {u: a_ab[u].astype(BF16) for u in units}
    pw = {u: _mm(pw[u], stack(pw[u])).astype(BF16) for u in units}
    for lvl in range(1, RWKV_SOLVE_LEVELS):
        for u in units:
            t_b = tinv[u].astype(BF16)
            if lvl + 1 < RWKV_SOLVE_LEVELS:
                prod = _mm(pw[u], stack(jnp.concatenate([t_b, pw[u]], axis=1)))
                tinv[u] = tinv[u] + prod[:, :W2]
                pw[u] = prod[:, W2:].astype(BF16)
            else:
                tinv[u] = tinv[u] + _mm(pw[u], stack(t_b))
    sol = {u: _mm(tinv[u], stack(jnp.concatenate([sl(at_b, u), x1[u].astype(BF16)], axis=1))).astype(BF16)
           for u in units}

    qp, yp, gbd, hp = {}, {}, {}, {}
    for u in units:
        v_u = sl(v_b, u)
        rhs_a = jnp.concatenate([stack(sol[u]), jnp.concatenate([zeros_2l, stack(v_u)], axis=1)], axis=0)
        m2a = _mm(a_r[u], rhs_a)
        rhs_b = jnp.concatenate([sol[u], jnp.concatenate([zeros_l, v_u], axis=1)], axis=0)
        m2b = _mm(jnp.concatenate([sl(bh_b, u), sl(kh_b, u)], axis=0), rhs_b, _TN)
        m2b = jnp.where(same_head, m2b, 0.0)
        qp[u] = sl(rt, u) + m2a[:, :W2]
        yp[u] = m2a[:, W2:]
        c, p = u
        gbd[u] = jnp.where(eye_full, p_end[c * L:c * L + 1, p * W2:(p + 1) * W2], 0.0) + m2b[:, :W2]
        hp[u] = m2b[:, W2:]

    ys = {}
    for u in units:
        m3 = _mm(jnp.concatenate([qp[u], gbd[u]], axis=0), state_ref[u[1]])
        ys[u] = m3[:L] + yp[u]
        state_ref[u[1]] = m3[L:] + hp[u]

    y = jnp.concatenate([jnp.concatenate([ys[(c, p)] for p in range(A_WIDTH // W2)], axis=1)
                         for c in range(tb // L)], axis=0)
    inv_n = 1.0 / hd
    y_sum, rk_sum = _seg_sum_two(y, r * k * rk_ref[...], seg2_ref)
    yc = y - y_sum * inv_n
    var = _seg_sum(yc * yc, seg_ref) * inv_n
    yn = yc * lax.rsqrt(var + A_LN_EPS) * ln_ref[0:1, :] + ln_ref[1:2, :]
    y_ref[...] = ((yn + rk_sum * v) * g).astype(y_ref.dtype)


def _rwkv(za, batch, mu, w0, w_up, a0, a_up, g_up, k_k, k_a, r_k, ln, seg, seg2, tri):
    n = za.shape[0]
    nt = n // batch // SEQ_TB
    row = lambda p: p.reshape(1, -1)
    consts = [row(mu), row(w0), w_up, row(a0), a_up, g_up, row(k_k), row(k_a), row(r_k), ln, seg, seg2, tri]
    return pl.pallas_call(
        _rwkv_kernel,
        grid=(batch, nt),
        in_specs=[pl.BlockSpec((SEQ_TB, A_SHIFT_WIDTH), lambda b, t: (b * nt + t, 0))]
        + [_const_spec(c.shape) for c in consts],
        out_specs=pl.BlockSpec((SEQ_TB, A_WIDTH), lambda b, t: (b * nt + t, 0)),
        out_shape=jax.ShapeDtypeStruct((n, A_WIDTH), BF16),
        scratch_shapes=[pltpu.VMEM((1, A_SHIFT_WIDTH), F32),
                        pltpu.VMEM((A_HEADS // 2, 2 * A_HEAD_DIM, 2 * A_HEAD_DIM), F32)],
        compiler_params=_params("parallel", "arbitrary"),
        name="rwkv",
    )(za, *consts)


def _rope_kernel(pos_ref, invf_ref, sign_ref, cos_ref, sin_ref):
    ang = pos_ref[...].astype(F32) * invf_ref[...]
    cos_ref[...] = jnp.cos(ang)
    sin_ref[...] = jnp.sin(ang) * sign_ref[...]


def _rope_tables(pos, invf, sign):
    n = pos.shape[0]
    return pl.pallas_call(
        _rope_kernel,
        grid=(n // ROPE_TM,),
        in_specs=[pl.BlockSpec((ROPE_TM, 1), lambda i: (i, 0)), _const_spec((1, LANES)), _const_spec((1, LANES))],
        out_specs=[pl.BlockSpec((ROPE_TM, LANES), lambda i: (i, 0))] * 2,
        out_shape=[jax.ShapeDtypeStruct((n, LANES), F32)] * 2,
        compiler_params=_params("parallel"),
        name="rope",
    )(pos, invf, sign)


def _ret_kernel(zb_ref, cos_ref, sin_ref, dmask_ref, qdec_ref, kdec_ref, cdec_ref, gn_ref, segm_ref, y_ref, state_ref):
    C = B_CHUNK
    dk, dv = B_QK_DIM, B_V_DIM
    QW, VW = 2 * dk, 2 * dv
    assert QW == LANES and dv == LANES

    @pl.when(pl.program_id(1) == 0)
    def _():
        state_ref[...] = jnp.zeros_like(state_ref)

    def iota(shape, dim):
        return lax.broadcasted_iota(jnp.int32, shape, dim)

    first_half = (iota((C, B_QK_WIDTH), 1) & (dk - 1)) < dk // 2
    first_q = iota((C, QW), 1) < dk
    first_v = iota((C, VW), 1) < dv
    same_head = (iota((QW, VW), 0) < dk) == (iota((QW, VW), 1) < dv)

    def rotate(x, cos, sin):
        partner = jnp.where(first_half, pltpu.roll(x, B_QK_WIDTH - dk // 2, 1), pltpu.roll(x, dk // 2, 1))
        return x * cos + partner * sin

    def stack(x, first):
        return jnp.concatenate([jnp.where(first, x, 0.0), jnp.where(first, 0.0, x)], axis=0)

    chunks = range(zb_ref.shape[0] // C)
    q, k, qd, kd = {}, {}, {}, {}
    for c in chunks:
        rs = slice(c * C, (c + 1) * C)
        reps = B_QK_WIDTH // LANES
        cos = jnp.concatenate([cos_ref[rs, :]] * reps, axis=1)
        sin = jnp.concatenate([sin_ref[rs, :]] * reps, axis=1)
        q[c] = rotate(zb_ref[rs, 0:B_QK_WIDTH], cos, sin)
        k[c] = rotate(zb_ref[rs, B_QK_WIDTH:2 * B_QK_WIDTH], cos, sin) * (dk ** -0.5)
        qd[c] = q[c] * qdec_ref[...]
        kd[c] = k[c] * kdec_ref[...]

    units = [(c, p) for c in chunks for p in range(B_HEADS // 2)]

    def qk(x, u):
        return x[u[0]][:, u[1] * QW:(u[1] + 1) * QW]

    def vg(base, u):
        c, p = u
        return zb_ref[c * C:(c + 1) * C, base + p * VW:base + (p + 1) * VW]

    v = {u: vg(2 * B_QK_WIDTH, u) for u in units}
    scores = {u: _mm(qk(q, u), stack(qk(k, u), first_q), _NT) * dmask_ref[u[1]] for u in units}
    inner = {u: _mm(scores[u], stack(v[u], first_v)) for u in units}
    kv = {u: jnp.where(same_head, _mm(qk(kd, u), v[u], _TN), 0.0) for u in units}

    before = {}
    for p in range(B_HEADS // 2):
        st = state_ref[p]
        for c in chunks:
            before[(c, p)] = st
            st = st * cdec_ref[p] + kv[(c, p)]
        state_ref[p] = st

    y = {u: inner[u] + _mm(qk(qd, u), before[u]) for u in units}
    yc = {u: y[u] - _mm(y[u], segm_ref[...]) for u in units}
    var = {u: _mm(yc[u] * yc[u], segm_ref[...]) for u in units}
    for u in units:
        c, p = u
        g = vg(2 * B_QK_WIDTH + B_V_WIDTH, u)
        out = yc[u] * lax.rsqrt(var[u] + B_GN_EPS) * gn_ref[:, p * VW:(p + 1) * VW] * (g * _sigmoid(g))
        y_ref[c * C:(c + 1) * C, p * VW:(p + 1) * VW] = out.astype(y_ref.dtype)


def _retention_tables():
    heads = np.arange(B_HEADS, dtype=np.float64)
    log_gamma = np.log1p(-np.exp2(-5.0 - heads))
    pos = np.arange(B_CHUNK, dtype=np.float64)
    diff = pos[:, None] - pos[None, :]
    dmask = np.where(diff[None] >= 0, np.exp(np.maximum(diff, 0.0)[None] * log_gamma[:, None, None]), 0.0)
    dmask = dmask.reshape(B_HEADS // 2, 2, B_CHUNK, B_CHUNK).transpose(0, 2, 1, 3).reshape(B_HEADS // 2, B_CHUNK, -1)
    q_decay = np.exp((pos + 1.0)[:, None] * log_gamma[None, :])
    k_decay = np.exp((B_CHUNK - 1.0 - pos)[:, None] * log_gamma[None, :])
    chunk_decay = np.exp(B_CHUNK * log_gamma).reshape(B_HEADS // 2, 2, 1, 1)
    cdec = np.broadcast_to(chunk_decay, (B_HEADS // 2, 2, B_QK_DIM, 2 * B_V_DIM)).reshape(B_HEADS // 2, 2 * B_QK_DIM, -1)
    segm = np.kron(np.eye(2), np.full((B_V_DIM, B_V_DIM), 1.0 / B_V_DIM))
    rep = lambda t: np.repeat(t, B_QK_DIM, axis=1)
    return (jnp.asarray(dmask, F32), jnp.asarray(rep(q_decay), F32), jnp.asarray(rep(k_decay), F32),
            jnp.asarray(cdec, F32), jnp.asarray(segm, BF16))


def _retention(zb, batch, cos, sin, gn):
    n = zb.shape[0]
    nt = n // batch // SEQ_TB
    dmask, qdec, kdec, cdec, segm = _retention_tables()
    consts = [dmask, qdec, kdec, cdec, gn.reshape(1, -1), segm]
    tok = lambda wd: pl.BlockSpec((SEQ_TB, wd), lambda b, t: (b * nt + t, 0))
    return pl.pallas_call(
        _ret_kernel,
        grid=(batch, nt),
        in_specs=[tok(B_IN_WIDTH), tok(LANES), tok(LANES)] + [_const_spec(c.shape) for c in consts],
        out_specs=tok(B_V_WIDTH),
        out_shape=jax.ShapeDtypeStruct((n, B_V_WIDTH), BF16),
        scratch_shapes=[pltpu.VMEM((B_HEADS // 2, 2 * B_QK_DIM, 2 * B_V_DIM), F32)],
        compiler_params=_params("parallel", "arbitrary"),
        name="retention",
    )(zb, cos, sin, *consts)


def _merge_kernel(h_ref, ya_ref, yb_ref, gp_ref, gb_ref, pa_ref, pb_ref, wo_ref, o_ref):
    d = h_ref.shape[1]
    gates = _sigmoid(gp_ref[...] + gb_ref[...])
    merged = (gates[:, :d] * _mm(ya_ref[...], pa_ref[...]) + gates[:, d:] * _mm(yb_ref[...], pb_ref[...]))
    o_ref[...] = h_ref[...] + _mm(merged, wo_ref[...])


def _merge(h, ya, yb, gp, gate_b, p_a, p_b, w_o):
    n, d = h.shape
    tok = lambda wd: pl.BlockSpec((MERGE_TM, wd), lambda i: (i, 0))
    return pl.pallas_call(
        _merge_kernel,
        grid=(n // MERGE_TM,),
        in_specs=[tok(d), tok(A_WIDTH), tok(B_V_WIDTH), tok(2 * d), _const_spec((1, 2 * d)),
                  _const_spec(p_a.shape), _const_spec(p_b.shape), _const_spec(w_o.shape)],
        out_specs=tok(d),
        out_shape=jax.ShapeDtypeStruct((n, d), F32),
        compiler_params=_params("parallel"),
        name="merge",
    )(h, ya, yb, gp, gate_b.reshape(1, 2 * d), p_a, p_b, w_o)


def _memkv_kernel(m_ref, g_ref, w_ref, o_ref):
    o_ref[...] = _mm(_rms(m_ref[...], g_ref[...]), w_ref[...]).astype(BF16)


def _memkv(mem, g, w_kv):
    n, d = mem.shape
    tm = 512
    return pl.pallas_call(
        _memkv_kernel,
        grid=(n // tm,),
        in_specs=[pl.BlockSpec((tm, d), lambda i: (i, 0)), _const_spec((1, d)), _const_spec(w_kv.shape)],
        out_specs=pl.BlockSpec((tm, 2 * d), lambda i: (i, 0)),
        out_shape=jax.ShapeDtypeStruct((n, 2 * d), BF16),
        compiler_params=_params("parallel"),
        name="memkv",
    )(mem, g.reshape(1, d), w_kv)


def _cross_kernel(h_ref, g_ref, wq_ref, kv_ref, wo_ref, o_ref):
    h = h_ref[...]
    d = h.shape[1]
    hd = d // X_HEADS
    q = _mm(_rms(h, g_ref[...]), wq_ref[...])
    heads = range(X_HEADS)
    s = [_mm(q[:, i * hd:(i + 1) * hd], kv_ref[:, i * hd:(i + 1) * hd], _NT) * (hd ** -0.5) for i in heads]
    e = [jnp.exp(s[i] - jnp.max(s[i], axis=-1, keepdims=True)) for i in heads]
    p = [e[i] * (1.0 / jnp.sum(e[i], axis=-1, keepdims=True)) for i in heads]
    outs = [_mm(p[i], kv_ref[:, d + i * hd:d + (i + 1) * hd]) for i in heads]
    o_ref[...] = h + _mm(jnp.concatenate(outs, axis=1), wo_ref[...])


def _cross(h, batch, g, w_q, kv, w_o):
    n, d = h.shape
    nt = n // batch // CROSS_TM
    mlen = kv.shape[0] // batch
    tok = pl.BlockSpec((CROSS_TM, d), lambda b, t: (b * nt + t, 0))
    return pl.pallas_call(
        _cross_kernel,
        grid=(batch, nt),
        in_specs=[tok, _const_spec((1, d)), _const_spec(w_q.shape),
                  pl.BlockSpec((mlen, 2 * d), lambda b, t: (b, 0)), _const_spec(w_o.shape)],
        out_specs=tok,
        out_shape=jax.ShapeDtypeStruct((n, d), F32),
        compiler_params=_params("parallel", "arbitrary"),
        name="cross",
    )(h, g.reshape(1, d), w_q, kv, w_o)


def kernel(x, mem, positions, norm_g, ffn_w_in, ffn_w_out, mix_w_in, mix_gate_b, shift_mu, a_w0, a_w_up, a_a0, a_a_up, a_g_up, a_k_k, a_k_a, a_r_k, a_ln, b_gn, w_branch_a, w_branch_b, mix_w_out, mem_norm, cross_w_q, cross_w_kv, cross_w_o, final_norm):
    batch, seq, d = x.shape
    depth = norm_g.shape[0]
    n = batch * seq
    assert seq % SEQ_TB == 0 and seq % CROSS_TM == 0 and n % ROPE_TM == 0
    assert n % FFN_TM == 0 and n % INPROJ_TM == 0 and n % MERGE_TM == 0
    assert SEQ_TB % B_CHUNK == 0 and SEQ_TB % TRI_ROWS == 0 and TRI_ROWS % RWKV_CHUNK == 0
    assert 2 ** RWKV_SOLVE_LEVELS == RWKV_CHUNK
    bf = lambda t: t.astype(BF16)

    half = B_QK_DIM // 2
    inv_freq = ROPE_BASE ** (-jnp.arange(half, dtype=F32) / half)
    invf = jnp.tile(inv_freq, LANES // half).reshape(1, LANES)
    lane = np.arange(LANES)
    sign = jnp.asarray(np.where((lane % B_QK_DIM) < half, -1.0, 1.0), F32).reshape(1, LANES)
    cos, sin = _rope_tables(positions.reshape(n, 1), invf, sign)

    seg = jnp.asarray(np.kron(np.eye(LANES // A_HEAD_DIM), np.ones((A_HEAD_DIM, A_HEAD_DIM))), BF16)
    seg2 = jnp.asarray(np.kron(np.eye(2 * LANES // A_HEAD_DIM), np.ones((A_HEAD_DIM, A_HEAD_DIM))), BF16)
    tri = jnp.asarray(np.kron(np.eye(TRI_ROWS // RWKV_CHUNK), np.tril(np.ones((RWKV_CHUNK, RWKV_CHUNK)))), BF16)
    mem2 = mem.reshape(-1, d)

    h = x.reshape(n, d)
    for l in range(depth):
        h = _ffn(h, norm_g[l, 0], bf(ffn_w_in[l, 0]), bf(ffn_w_out[l, 0]), final_norm, False)
        za, zb, gp = _inproj(h, norm_g[l, 1], bf(mix_w_in[l]))
        ya = _rwkv(za, batch, shift_mu[l], a_w0[l], bf(a_w_up[l]), a_a0[l], bf(a_a_up[l]), bf(a_g_up[l]),
                   a_k_k[l], a_k_a[l], a_r_k[l], a_ln[l], seg, seg2, tri)
        yb = _retention(zb, batch, cos, sin, b_gn[l])
        h = _merge(h, ya, yb, gp, mix_gate_b[l], bf(w_branch_a[l]), bf(w_branch_b[l]), bf(mix_w_out[l]))
        kv = _memkv(mem2, mem_norm[l], bf(cross_w_kv[l]))
        h = _cross(h, batch, norm_g[l, 2], bf(cross_w_q[l]), kv, bf(cross_w_o[l]))
        h = _ffn(h, norm_g[l, 3], bf(ffn_w_in[l, 1]), bf(ffn_w_out[l, 1]), final_norm, l == depth - 1)
    return h.reshape(batch, seq, d)
```

```python
import functools
import math

import jax
import jax.numpy as jnp
import numpy as np
from jax import lax
from jax.experimental import pallas as pl
from jax.experimental.pallas import tpu as pltpu

F32 = jnp.float32
BF16 = jnp.bfloat16

NORM_EPS = 1e-6
D_FF = 2816
A_HEADS = 8
A_HEAD_DIM = 64
A_WIDTH = A_HEADS * A_HEAD_DIM
A_RANK_W = 64
A_RANK_A = 64
A_RANK_G = 128
A_LN_EPS = 64e-5
A_SHIFT_WIDTH = 3 * A_WIDTH + A_RANK_W + A_RANK_A + A_RANK_G
B_HEADS = 8
B_QK_DIM = 64
B_V_DIM = 128
B_QK_WIDTH = B_HEADS * B_QK_DIM
B_V_WIDTH = B_HEADS * B_V_DIM
B_IN_WIDTH = 2 * B_QK_WIDTH + 2 * B_V_WIDTH
B_CHUNK = 128
B_GN_EPS = 1e-5
ROPE_BASE = 10000.0
X_HEADS = 4

RWKV_CHUNK = 64
RWKV_SOLVE_LEVELS = 6
LANES = 128
VMEM_LIMIT = 56 * 1024 * 1024

FFN_TM = 1024
FFN_TF = 256
INPROJ_TM = 512
SEQ_TB = 1024
RET_GROUP = 1
MERGE_TM = 1024
CROSS_TM = 1024
ROPE_TM = 1024
TRI_ROWS = 256


def _const_spec(shape):
    nd = len(shape)
    return pl.BlockSpec(shape, lambda *_: (0,) * nd, pipeline_mode=pl.Buffered(1))


def _params(*sem):
    return pltpu.CompilerParams(dimension_semantics=sem, vmem_limit_bytes=VMEM_LIMIT)


def _pieces(x, n):
    if x.dtype == BF16:
        return [x]
    out, rem = [], x
    for i in range(n):
        p = rem.astype(BF16)
        out.append(p)
        if i + 1 < n:
            rem = rem - p.astype(F32)
    return out


_NN = (((1,), (0,)), ((), ()))
_NT = (((1,), (1,)), ((), ()))
_TN = (((0,), (0,)), ((), ()))


def _mm(a, b, dims=_NN, pa=1, pb=1):
    acc = None
    for x in _pieces(a, pa):
        for y in _pieces(b, pb):
            t = lax.dot_general(x, y, dims, preferred_element_type=F32)
            acc = t if acc is None else acc + t
    return acc


def _rms(x, g):
    ms = jnp.mean(x * x, axis=-1, keepdims=True)
    return x * lax.rsqrt(ms + NORM_EPS) * g


def _sigmoid(x):
    return 0.5 * jnp.tanh(0.5 * x) + 0.5


def _ffn_kernel(h_ref, g_ref, wg_ref, wu_ref, wo_ref, fg_ref, o_ref, act_ref, *, final):
    h = h_ref[...]
    u = _rms(h, g_ref[...]).astype(BF16)
    for c in range(D_FF // FFN_TF):
        sl = pl.ds(c * FFN_TF, FFN_TF)
        gate = jnp.dot(u, wg_ref[:, sl], preferred_element_type=F32)
        up = jnp.dot(u, wu_ref[:, sl], preferred_element_type=F32)
        act_ref[:, sl] = (gate * _sigmoid(gate) * up).astype(BF16)
    y = jnp.dot(act_ref[...], wo_ref[...], preferred_element_type=F32)
    out = h + 0.5 * y
    if final:
        out = _rms(out, fg_ref[...])
    o_ref[...] = out


def _ffn(h, g, w_in, w_out, final_g, final):
    n, d = h.shape
    return pl.pallas_call(
        functools.partial(_ffn_kernel, final=final),
        grid=(n // FFN_TM,),
        in_specs=[
            pl.BlockSpec((FFN_TM, d), lambda i: (i, 0)),
            _const_spec((1, d)),
            pl.BlockSpec((d, D_FF), lambda i: (0, 0), pipeline_mode=pl.Buffered(1)),
            pl.BlockSpec((d, D_FF), lambda i: (0, 1), pipeline_mode=pl.Buffered(1)),
            _const_spec((D_FF, d)),
            _const_spec((1, d)),
        ],
        out_specs=pl.BlockSpec((FFN_TM, d), lambda i: (i, 0)),
        out_shape=jax.ShapeDtypeStruct((n, d), F32),
        scratch_shapes=[pltpu.VMEM((FFN_TM, D_FF), BF16)],
        compiler_params=_params("parallel"),
        name="ffn",
    )(h, g.reshape(1, d), w_in, w_in, w_out, final_g.reshape(1, d))


def _inproj_kernel(h_ref, g_ref, w_ref, za_ref, zb_ref, gp_ref):
    u = _rms(h_ref[...], g_ref[...]).astype(BF16)
    c0, c1 = A_SHIFT_WIDTH, A_SHIFT_WIDTH + B_IN_WIDTH
    za_ref[...] = jnp.dot(u, w_ref[:, :c0], preferred_element_type=F32)
    zb_ref[...] = jnp.dot(u, w_ref[:, c0:c1], preferred_element_type=F32)
    gp_ref[...] = jnp.dot(u, w_ref[:, c1:], preferred_element_type=F32).astype(gp_ref.dtype)


def _inproj(h, g, w):
    n, d = h.shape
    widths = (A_SHIFT_WIDTH, B_IN_WIDTH, w.shape[1] - A_SHIFT_WIDTH - B_IN_WIDTH)
    return pl.pallas_call(
        _inproj_kernel,
        grid=(n // INPROJ_TM,),
        in_specs=[
            pl.BlockSpec((INPROJ_TM, d), lambda i: (i, 0)),
            _const_spec((1, d)),
            _const_spec(w.shape),
        ],
        out_specs=[pl.BlockSpec((INPROJ_TM, wd), lambda i: (i, 0)) for wd in widths],
        out_shape=[jax.ShapeDtypeStruct((n, wd), dt) for wd, dt in zip(widths, (F32, F32, BF16))],
        compiler_params=_params("parallel"),
        name="inproj",
    )(h, g.reshape(1, d), w)


def _seg_sum(x, seg_ref):
    wd = seg_ref.shape[0]
    return jnp.concatenate([_mm(x[:, i:i + wd], seg_ref[...]) for i in range(0, x.shape[1], wd)], axis=1)


def _seg_sum_two(x1, x2, seg2_ref):
    wd = seg2_ref.shape[0] // 2
    o1, o2 = [], []
    for i in range(0, x1.shape[1], wd):
        s = _mm(jnp.concatenate([x1[:, i:i + wd], x2[:, i:i + wd]], axis=1), seg2_ref[...])
        o1.append(s[:, :wd])
        o2.append(s[:, wd:])
    return jnp.concatenate(o1, axis=1), jnp.concatenate(o2, axis=1)


def _rwkv_kernel(za_ref, mu_ref, w0_ref, wup_ref, a0_ref, aup_ref, gup_ref, kk_ref, ka_ref, rk_ref, ln_ref,
                 seg_ref, seg2_ref, tri_ref, y_ref, carry_ref, state_ref):
    L = RWKV_CHUNK
    hd = A_HEAD_DIM
    W2 = 2 * hd
    assert W2 == LANES and L == hd

    @pl.when(pl.program_id(1) == 0)
    def _():
        carry_ref[...] = jnp.zeros_like(carry_ref)
        state_ref[...] = jnp.zeros_like(state_ref)

    z = za_ref[...]
    tb = z.shape[0]
    row = lax.broadcasted_iota(jnp.int32, (tb, 1), 0)
    prev = jnp.where(row == 0, carry_ref[...], pltpu.roll(z, 1, 0))
    carry_ref[...] = z[tb - 1:tb, :]
    xs = z + (prev - z) * mu_ref[...]

    r = xs[:, 0:A_WIDTH]
    k = xs[:, A_WIDTH:2 * A_WIDTH]
    v = xs[:, 2 * A_WIDTH:3 * A_WIDTH]
    o = 3 * A_WIDTH
    w_lo = xs[:, o:o + A_RANK_W]
    a_lo = xs[:, o + A_RANK_W:o + A_RANK_W + A_RANK_A]
    g_lo = xs[:, o + A_RANK_W + A_RANK_A:]

    w = w0_ref[...] + _mm(jnp.tanh(w_lo), wup_ref[...])
    logd = -_sigmoid(w) * math.exp(-0.5)
    a = _sigmoid(a0_ref[...] + _mm(a_lo, aup_ref[...]))
    g = _mm(_sigmoid(g_lo), gup_ref[...])

    kk = k * kk_ref[...]
    kk = kk * lax.rsqrt(jnp.maximum(_seg_sum(kk * kk, seg_ref), 1e-24))
    k = k * (1.0 + (a - 1.0) * ka_ref[...])
    kka = kk * a

    tr = tri_ref.shape[0]
    cum = jnp.concatenate([_mm(tri_ref[...], logd[i:i + tr], pb=2) for i in range(0, tb, tr)], axis=0)
    tot = jnp.concatenate([jnp.broadcast_to(cum[c + L - 1:c + L, :], (L, A_WIDTH)) for c in range(0, tb, L)], axis=0)
    e_inc = jnp.exp(cum)
    e_exc = jnp.exp(cum - logd)
    e_neg = jnp.exp(-cum)
    e_end = jnp.exp(tot - cum)
    p_end = jnp.exp(tot)
    rt = r * e_inc
    at = -kk * e_exc
    bt = kka * e_neg
    kt = k * e_neg
    bh = kka * e_end
    kh = k * e_end

    def iota(shape, dim):
        return lax.broadcasted_iota(jnp.int32, shape, dim)

    first1 = (iota((L, W2), 1) & (W2 - 1)) < hd
    first2 = (iota((L, 2 * W2), 1) & (W2 - 1)) < hd
    strict_lower = (iota((L, 2 * W2), 1) & (L - 1)) < iota((L, 2 * W2), 0)
    lower = (iota((L, 2 * W2), 1) & (L - 1)) <= iota((L, 2 * W2), 0)
    eye_pair = jnp.where((iota((L, W2), 1) & (L - 1)) == iota((L, W2), 0), 1.0, 0.0)
    eye_full = iota((W2, W2), 0) == iota((W2, W2), 1)
    same_head = (iota((W2, 2 * W2), 0) < hd) == ((iota((W2, 2 * W2), 1) & (W2 - 1)) < hd)
    zeros_l = jnp.zeros((L, W2), BF16)
    zeros_2l = jnp.zeros((2 * L, W2), BF16)

    def stack(x):
        first = first1 if x.shape[1] == W2 else first2
        return jnp.concatenate([jnp.where(first, x, 0.0), jnp.where(first, 0.0, x)], axis=0)

    units = [(c, p) for c in range(tb // L) for p in range(A_WIDTH // W2)]

    def sl(x, u):
        c, p = u
        return x[c * L:(c + 1) * L, p * W2:(p + 1) * W2]

    at_b, rt_b, bt_b, kt_b, bh_b, kh_b, v_b = (x.astype(BF16) for x in (at, rt, bt, kt, bh, kh, v))

    a_ab, a_ak, a_r = {}, {}, {}
    for u in units:
        lhs = jnp.concatenate([sl(at_b, u), sl(rt_b, u)], axis=0)
        rhs = jnp.concatenate([stack(sl(bt_b, u)), stack(sl(kt_b, u))], axis=0)
        m1 = _mm(lhs, rhs, _NT)
        a_a = jnp.where(strict_lower, m1[:L], 0.0)
        a_ab[u] = a_a[:, :W2]
        a_ak[u] = a_a[:, W2:].astype(BF16)
        a_r[u] = jnp.where(lower, m1[L:], 0.0).astype(BF16)
    x1 = {u: _mm(a_ak[u], stack(sl(v_b, u))) for u in units}

    tinv = {u: eye_pair + a_ab[u] for u in units}
    pw = {u: a_ab[u].astype(BF16) for u in units}
    pw = {u: _mm(pw[u], stack(pw[u])).astype(BF16) for u in units}
    for lvl in range(1, RWKV_SOLVE_LEVELS):
        for u in units:
            t_b = tinv[u].astype(BF16)
            if lvl + 1 < RWKV_SOLVE_LEVELS:
                prod = _mm(pw[u], stack(jnp.concatenate([t_b, pw[u]], axis=1)))
                tinv[u] = tinv[u] + prod[:, :W2]
                pw[u] = prod[:, W2:].astype(BF16)
            else:
                tinv[u] = tinv[u] + _mm(pw[u], stack(t_b))
    sol = {u: _mm(tinv[u], stack(jnp.concatenate([sl(at_b, u), x1[u].astype(BF16)], axis=1))).astype(BF16)
           for u in units}

    qp, yp, gbd, hp = {}, {}, {}, {}
    for u in units:
        v_u = sl(v_b, u)
        rhs_a = jnp.concatenate([stack(sol[u]), jnp.concatenate([zeros_2l, stack(v_u)], axis=1)], axis=0)
        m2a = _mm(a_r[u], rhs_a)
        rhs_b = jnp.concatenate([sol[u], jnp.concatenate([zeros_l, v_u], axis=1)], axis=0)
        m2b = _mm(jnp.concatenate([sl(bh_b, u), sl(kh_b, u)], axis=0), rhs_b, _TN)
        m2b = jnp.where(same_head, m2b, 0.0)
        qp[u] = sl(rt, u) + m2a[:, :W2]
        yp[u] = m2a[:, W2:]
        c, p = u
        gbd[u] = jnp.where(eye_full, p_end[c * L:c * L + 1, p * W2:(p + 1) * W2], 0.0) + m2b[:, :W2]
        hp[u] = m2b[:, W2:]

    ys = {}
    for u in units:
        m3 = _mm(jnp.concatenate([qp[u], gbd[u]], axis=0), state_ref[u[1]])
        ys[u] = m3[:L] + yp[u]
        state_ref[u[1]] = m3[L:] + hp[u]

    y = jnp.concatenate([jnp.concatenate([ys[(c, p)] for p in range(A_WIDTH // W2)], axis=1)
                         for c in range(tb // L)], axis=0)
    inv_n = 1.0 / hd
    y_sum, rk_sum = _seg_sum_two(y, r * k * rk_ref[...], seg2_ref)
    yc = y - y_sum * inv_n
    var = _seg_sum(yc * yc, seg_ref) * inv_n
    yn = yc * lax.rsqrt(var + A_LN_EPS) * ln_ref[0:1, :] + ln_ref[1:2, :]
    y_ref[...] = ((yn + rk_sum * v) * g).astype(y_ref.dtype)


def _rwkv(za, batch, mu, w0, w_up, a0, a_up, g_up, k_k, k_a, r_k, ln, seg, seg2, tri):
    n = za.shape[0]
    nt = n // batch // SEQ_TB
    row = lambda p: p.reshape(1, -1)
    consts = [row(mu), row(w0), w_up, row(a0), a_up, g_up, row(k_k), row(k_a), row(r_k), ln, seg, seg2, tri]
    return pl.pallas_call(
        _rwkv_kernel,
        grid=(batch, nt),
        in_specs=[pl.BlockSpec((SEQ_TB, A_SHIFT_WIDTH), lambda b, t: (b * nt + t, 0))]
        + [_const_spec(c.shape) for c in consts],
        out_specs=pl.BlockSpec((SEQ_TB, A_WIDTH), lambda b, t: (b * nt + t, 0)),
        out_shape=jax.ShapeDtypeStruct((n, A_WIDTH), BF16),
        scratch_shapes=[pltpu.VMEM((1, A_SHIFT_WIDTH), F32),
                        pltpu.VMEM((A_HEADS // 2, 2 * A_HEAD_DIM, 2 * A_HEAD_DIM), F32)],
        compiler_params=_params("parallel", "arbitrary"),
        name="rwkv",
    )(za, *consts)


def _rope_kernel(pos_ref, invf_ref, sign_ref, cos_ref, sin_ref):
    ang = pos_ref[...].astype(F32) * invf_ref[...]
    cos_ref[...] = jnp.cos(ang)
    sin_ref[...] = jnp.sin(ang) * sign_ref[...]


def _rope_tables(pos, invf, sign):
    n = pos.shape[0]
    return pl.pallas_call(
        _rope_kernel,
        grid=(n // ROPE_TM,),
        in_specs=[pl.BlockSpec((ROPE_TM, 1), lambda i: (i, 0)), _const_spec((1, LANES)), _const_spec((1, LANES))],
        out_specs=[pl.BlockSpec((ROPE_TM, LANES), lambda i: (i, 0))] * 2,
        out_shape=[jax.ShapeDtypeStruct((n, LANES), F32)] * 2,
        compiler_params=_params("parallel"),
        name="rope",
    )(pos, invf, sign)


def _ret_kernel(zb_ref, cos_ref, sin_ref, dmask_ref, qdec_ref, kdec_ref, cdec_ref, gn_ref, segm_ref, y_ref, state_ref):
    C = B_CHUNK
    dk, dv = B_QK_DIM, B_V_DIM
    QW, VW = 2 * dk, 2 * dv
    assert QW == LANES and dv == LANES

    @pl.when(pl.program_id(1) == 0)
    def _():
        state_ref[...] = jnp.zeros_like(state_ref)

    def iota(shape, dim):
        return lax.broadcasted_iota(jnp.int32, shape, dim)

    first_half = (iota((C, B_QK_WIDTH), 1) & (dk - 1)) < dk // 2
    first_q = iota((C, QW), 1) < dk
    first_v = iota((C, VW), 1) < dv
    same_head = (iota((QW, VW), 0) < dk) == (iota((QW, VW), 1) < dv)

    def rotate(x, cos, sin):
        partner = jnp.where(first_half, pltpu.roll(x, B_QK_WIDTH - dk // 2, 1), pltpu.roll(x, dk // 2, 1))
        return x * cos + partner * sin

    def stack(x, first):
        return jnp.concatenate([jnp.where(first, x, 0.0), jnp.where(first, 0.0, x)], axis=0)

    chunks = range(zb_ref.shape[0] // C)
    q, k, qd, kd = {}, {}, {}, {}
    for c in chunks:
        rs = slice(c * C, (c + 1) * C)
        reps = B_QK_WIDTH // LANES
        cos = jnp.concatenate([cos_ref[rs, :]] * reps, axis=1)
        sin = jnp.concatenate([sin_ref[rs, :]] * reps, axis=1)
        q[c] = rotate(zb_ref[rs, 0:B_QK_WIDTH], cos, sin)
        k[c] = rotate(zb_ref[rs, B_QK_WIDTH:2 * B_QK_WIDTH], cos, sin) * (dk ** -0.5)
        qd[c] = q[c] * qdec_ref[...]
        kd[c] = k[c] * kdec_ref[...]

    def qk(x, u):
        return x[u[0]][:, u[1] * QW:(u[1] + 1) * QW]

    def vg(base, u):
        c, p = u
        return zb_ref[c * C:(c + 1) * C, base + p * VW:base + (p + 1) * VW]

    for g0 in range(0, len(chunks), RET_GROUP):
        group = range(g0, g0 + RET_GROUP)
        units = [(c, p) for c in group for p in range(B_HEADS // 2)]
        v = {u: vg(2 * B_QK_WIDTH, u) for u in units}
        scores = {u: _mm(qk(q, u), stack(qk(k, u), first_q), _NT) * dmask_ref[u[1]] for u in units}
        inner = {u: _mm(scores[u], stack(v[u], first_v)) for u in units}
        kv = {u: jnp.where(same_head, _mm(qk(kd, u), v[u], _TN), 0.0) for u in units}

        before = {}
        for p in range(B_HEADS // 2):
            st = state_ref[p]
            for c in group:
                before[(c, p)] = st
                st = st * cdec_ref[p] + kv[(c, p)]
            state_ref[p] = st

        y = {u: inner[u] + _mm(qk(qd, u), before[u]) for u in units}
        yc = {u: y[u] - _mm(y[u], segm_ref[...]) for u in units}
        var = {u: _mm(yc[u] * yc[u], segm_ref[...]) for u in units}
        for u in units:
            c, p = u
            g = vg(2 * B_QK_WIDTH + B_V_WIDTH, u)
            out = yc[u] * lax.rsqrt(var[u] + B_GN_EPS) * gn_ref[:, p * VW:(p + 1) * VW] * (g * _sigmoid(g))
            y_ref[c * C:(c + 1) * C, p * VW:(p + 1) * VW] = out.astype(y_ref.dtype)


def _retention_tables():
    heads = np.arange(B_HEADS, dtype=np.float64)
    log_gamma = np.log1p(-np.exp2(-5.0 - heads))
    pos = np.arange(B_CHUNK, dtype=np.float64)
    diff = pos[:, None] - pos[None, :]
    dmask = np.where(diff[None] >= 0, np.exp(np.maximum(diff, 0.0)[None] * log_gamma[:, None, None]), 0.0)
    dmask = dmask.reshape(B_HEADS // 2, 2, B_CHUNK, B_CHUNK).transpose(0, 2, 1, 3).reshape(B_HEADS // 2, B_CHUNK, -1)
    q_decay = np.exp((pos + 1.0)[:, None] * log_gamma[None, :])
    k_decay = np.exp((B_CHUNK - 1.0 - pos)[:, None] * log_gamma[None, :])
    chunk_decay = np.exp(B_CHUNK * log_gamma).reshape(B_HEADS // 2, 2, 1, 1)
    cdec = np.broadcast_to(chunk_decay, (B_HEADS // 2, 2, B_QK_DIM, 2 * B_V_DIM)).reshape(B_HEADS // 2, 2 * B_QK_DIM, -1)
    segm = np.kron(np.eye(2), np.full((B_V_DIM, B_V_DIM), 1.0 / B_V_DIM))
    rep = lambda t: np.repeat(t, B_QK_DIM, axis=1)
    return (jnp.asarray(dmask, F32), jnp.asarray(rep(q_decay), F32), jnp.asarray(rep(k_decay), F32),
            jnp.asarray(cdec, F32), jnp.asarray(segm, BF16))


def _retention(zb, batch, cos, sin, gn):
    n = zb.shape[0]
    nt = n // batch // SEQ_TB
    dmask, qdec, kdec, cdec, segm = _retention_tables()
    consts = [dmask, qdec, kdec, cdec, gn.reshape(1, -1), segm]
    tok = lambda wd: pl.BlockSpec((SEQ_TB, wd), lambda b, t: (b * nt + t, 0))
    return pl.pallas_call(
        _ret_kernel,
        grid=(batch, nt),
        in_specs=[tok(B_IN_WIDTH), tok(LANES), tok(LANES)] + [_const_spec(c.shape) for c in consts],
        out_specs=tok(B_V_WIDTH),
        out_shape=jax.ShapeDtypeStruct((n, B_V_WIDTH), BF16),
        scratch_shapes=[pltpu.VMEM((B_HEADS // 2, 2 * B_QK_DIM, 2 * B_V_DIM), F32)],
        compiler_params=_params("parallel", "arbitrary"),
        name="retention",
    )(zb, cos, sin, *consts)


def _merge_kernel(h_ref, ya_ref, yb_ref, gp_ref, gb_ref, pa_ref, pb_ref, wo_ref, o_ref):
    d = h_ref.shape[1]
    gates = _sigmoid(gp_ref[...] + gb_ref[...])
    merged = (gates[:, :d] * _mm(ya_ref[...], pa_ref[...]) + gates[:, d:] * _mm(yb_ref[...], pb_ref[...]))
    o_ref[...] = h_ref[...] + _mm(merged, wo_ref[...])


def _merge(h, ya, yb, gp, gate_b, p_a, p_b, w_o):
    n, d = h.shape
    tok = lambda wd: pl.BlockSpec((MERGE_TM, wd), lambda i: (i, 0))
    return pl.pallas_call(
        _merge_kernel,
        grid=(n // MERGE_TM,),
        in_specs=[tok(d), tok(A_WIDTH), tok(B_V_WIDTH), tok(2 * d), _const_spec((1, 2 * d)),
                  _const_spec(p_a.shape), _const_spec(p_b.shape), _const_spec(w_o.shape)],
        out_specs=tok(d),
        out_shape=jax.ShapeDtypeStruct((n, d), F32),
        compiler_params=_params("parallel"),
        name="merge",
    )(h, ya, yb, gp, gate_b.reshape(1, 2 * d), p_a, p_b, w_o)


def _memkv_kernel(m_ref, g_ref, w_ref, o_ref):
    o_ref[...] = _mm(_rms(m_ref[...], g_ref[...]), w_ref[...]).astype(BF16)


def _memkv(mem, g, w_kv):
    n, d = mem.shape
    tm = 512
    return pl.pallas_call(
        _memkv_kernel,
        grid=(n // tm,),
        in_specs=[pl.BlockSpec((tm, d), lambda i: (i, 0)), _const_spec((1, d)), _const_spec(w_kv.shape)],
        out_specs=pl.BlockSpec((tm, 2 * d), lambda i: (i, 0)),
        out_shape=jax.ShapeDtypeStruct((n, 2 * d), BF16),
        compiler_params=_params("parallel"),
        name="memkv",
    )(mem, g.reshape(1, d), w_kv)


def _cross_kernel(h_ref, g_ref, wq_ref, kv_ref, wo_ref, o_ref):
    h = h_ref[...]
    d = h.shape[1]
    hd = d // X_HEADS
    q = _mm(_rms(h, g_ref[...]), wq_ref[...])
    heads = range(X_HEADS)
    s = [_mm(q[:, i * hd:(i + 1) * hd], kv_ref[:, i * hd:(i + 1) * hd], _NT) * (hd ** -0.5) for i in heads]
    e = [jnp.exp(s[i] - jnp.max(s[i], axis=-1, keepdims=True)) for i in heads]
    p = [e[i] * (1.0 / jnp.sum(e[i], axis=-1, keepdims=True)) for i in heads]
    outs = [_mm(p[i], kv_ref[:, d + i * hd:d + (i + 1) * hd]) for i in heads]
    o_ref[...] = h + _mm(jnp.concatenate(outs, axis=1), wo_ref[...])


def _cross(h, batch, g, w_q, kv, w_o):
    n, d = h.shape
    nt = n // batch // CROSS_TM
    mlen = kv.shape[0] // batch
    tok = pl.BlockSpec((CROSS_TM, d), lambda b, t: (b * nt + t, 0))
    return pl.pallas_call(
        _cross_kernel,
        grid=(batch, nt),
        in_specs=[tok, _const_spec((1, d)), _const_spec(w_q.shape),
                  pl.BlockSpec((mlen, 2 * d), lambda b, t: (b, 0)), _const_spec(w_o.shape)],
        out_specs=tok,
        out_shape=jax.ShapeDtypeStruct((n, d), F32),
        compiler_params=_params("parallel", "arbitrary"),
        name="cross",
    )(h, g.reshape(1, d), w_q, kv, w_o)


def kernel(x, mem, positions, norm_g, ffn_w_in, ffn_w_out, mix_w_in, mix_gate_b, shift_mu, a_w0, a_w_up, a_a0, a_a_up, a_g_up, a_k_k, a_k_a, a_r_k, a_ln, b_gn, w_branch_a, w_branch_b, mix_w_out, mem_norm, cross_w_q, cross_w_kv, cross_w_o, final_norm):
    batch, seq, d = x.shape
    depth = norm_g.shape[0]
    n = batch * seq
    assert seq % SEQ_TB == 0 and seq % CROSS_TM == 0 and n % ROPE_TM == 0
    assert n % FFN_TM == 0 and n % INPROJ_TM == 0 and n % MERGE_TM == 0
    assert SEQ_TB % B_CHUNK == 0 and SEQ_TB % TRI_ROWS == 0 and TRI_ROWS % RWKV_CHUNK == 0
    assert 2 ** RWKV_SOLVE_LEVELS == RWKV_CHUNK
    bf = lambda t: t.astype(BF16)

    half = B_QK_DIM // 2
    inv_freq = ROPE_BASE ** (-jnp.arange(half, dtype=F32) / half)
    invf = jnp.tile(inv_freq, LANES // half).reshape(1, LANES)
    lane = np.arange(LANES)
    sign = jnp.asarray(np.where((lane % B_QK_DIM) < half, -1.0, 1.0), F32).reshape(1, LANES)
    cos, sin = _rope_tables(positions.reshape(n, 1), invf, sign)

    seg = jnp.asarray(np.kron(np.eye(LANES // A_HEAD_DIM), np.ones((A_HEAD_DIM, A_HEAD_DIM))), BF16)
    seg2 = jnp.asarray(np.kron(np.eye(2 * LANES // A_HEAD_DIM), np.ones((A_HEAD_DIM, A_HEAD_DIM))), BF16)
    tri = jnp.asarray(np.kron(np.eye(TRI_ROWS // RWKV_CHUNK), np.tril(np.ones((RWKV_CHUNK, RWKV_CHUNK)))), BF16)
    mem2 = mem.reshape(-1, d)

    h = x.reshape(n, d)
    for l in range(depth):
        h = _ffn(h, norm_g[l, 0], bf(ffn_w_in[l, 0]), bf(ffn_w_out[l, 0]), final_norm, False)
        za, zb, gp = _inproj(h, norm_g[l, 1], bf(mix_w_in[l]))
        ya = _rwkv(za, batch, shift_mu[l], a_w0[l], bf(a_w_up[l]), a_a0[l], bf(a_a_up[l]), bf(a_g_up[l]),
                   a_k_k[l], a_k_a[l], a_r_k[l], a_ln[l], seg, seg2, tri)
        yb = _retention(zb, batch, cos, sin, b_gn[l])
        h = _merge(h, ya, yb, gp, mix_gate_b[l], bf(w_branch_a[l]), bf(w_branch_b[l]), bf(mix_w_out[l]))
        kv = _memkv(mem2, mem_norm[l], bf(cross_w_kv[l]))
        h = _cross(h, batch, norm_g[l, 2], bf(cross_w_q[l]), kv, bf(cross_w_o[l]))
        h = _ffn(h, norm_g[l, 3], bf(ffn_w_in[l, 1]), bf(ffn_w_out[l, 1]), final_norm, l == depth - 1)
    return h.reshape(batch, seq, d)
```

```python
import functools
import math

import jax
import jax.numpy as jnp
import numpy as np
from jax import lax
from jax.experimental import pallas as pl
from jax.experimental.pallas import tpu as pltpu

F32 = jnp.float32
BF16 = jnp.bfloat16

NORM_EPS = 1e-6
D_FF = 2816
A_HEADS = 8
A_HEAD_DIM = 64
A_WIDTH = A_HEADS * A_HEAD_DIM
A_RANK_W = 64
A_RANK_A = 64
A_RANK_G = 128
A_LN_EPS = 64e-5
A_SHIFT_WIDTH = 3 * A_WIDTH + A_RANK_W + A_RANK_A + A_RANK_G
B_HEADS = 8
B_QK_DIM = 64
B_V_DIM = 128
B_QK_WIDTH = B_HEADS * B_QK_DIM
B_V_WIDTH = B_HEADS * B_V_DIM
B_IN_WIDTH = 2 * B_QK_WIDTH + 2 * B_V_WIDTH
B_CHUNK = 128
B_GN_EPS = 1e-5
ROPE_BASE = 10000.0
X_HEADS = 4

RWKV_CHUNK = 64
RWKV_SOLVE_LEVELS = 6
LANES = 128
VMEM_LIMIT = 56 * 1024 * 1024

FFN_TM = 1024
FFN_TF = 256
INPROJ_TM = 512
SEQ_TB = 1024
RET_GROUP = 1
MERGE_TM = 1024
CROSS_TM = 1024
ROPE_TM = 1024
TRI_ROWS = 256


def _const_spec(shape):
    nd = len(shape)
    return pl.BlockSpec(shape, lambda *_: (0,) * nd, pipeline_mode=pl.Buffered(1))


def _params(*sem):
    return pltpu.CompilerParams(dimension_semantics=sem, vmem_limit_bytes=VMEM_LIMIT)


def _pieces(x, n):
    if x.dtype == BF16:
        return [x]
    out, rem = [], x
    for i in range(n):
        p = rem.astype(BF16)
        out.append(p)
        if i + 1 < n:
            rem = rem - p.astype(F32)
    return out


_NN = (((1,), (0,)), ((), ()))
_NT = (((1,), (1,)), ((), ()))
_TN = (((0,), (0,)), ((), ()))


def _mm(a, b, dims=_NN, pa=1, pb=1):
    acc = None
    for x in _pieces(a, pa):
        for y in _pieces(b, pb):
            t = lax.dot_general(x, y, dims, preferred_element_type=F32)
            acc = t if acc is None else acc + t
    return acc


def _rms(x, g):
    ms = jnp.mean(x * x, axis=-1, keepdims=True)
    return x * lax.rsqrt(ms + NORM_EPS) * g


def _sigmoid(x):
    return 0.5 * jnp.tanh(0.5 * x) + 0.5


def _ffn_kernel(h_ref, g_ref, wg_ref, wu_ref, wo_ref, fg_ref, o_ref, act_ref, *, final):
    h = h_ref[...]
    u = _rms(h, g_ref[...]).astype(BF16)
    for c in range(D_FF // FFN_TF):
        sl = pl.ds(c * FFN_TF, FFN_TF)
        gate = jnp.dot(u, wg_ref[:, sl], preferred_element_type=F32)
        up = jnp.dot(u, wu_ref[:, sl], preferred_element_type=F32)
        act_ref[:, sl] = (gate * _sigmoid(gate) * up).astype(BF16)
    y = jnp.dot(act_ref[...], wo_ref[...], preferred_element_type=F32)
    out = h + 0.5 * y
    if final:
        out = _rms(out, fg_ref[...])
    o_ref[...] = out


def _ffn(h, g, w_in, w_out, final_g, final):
    n, d = h.shape
    return pl.pallas_call(
        functools.partial(_ffn_kernel, final=final),
        grid=(n // FFN_TM,),
        in_specs=[
            pl.BlockSpec((FFN_TM, d), lambda i: (i, 0)),
            _const_spec((1, d)),
            pl.BlockSpec((d, D_FF), lambda i: (0, 0), pipeline_mode=pl.Buffered(1)),
            pl.BlockSpec((d, D_FF), lambda i: (0, 1), pipeline_mode=pl.Buffered(1)),
            _const_spec((D_FF, d)),
            _const_spec((1, d)),
        ],
        out_specs=pl.BlockSpec((FFN_TM, d), lambda i: (i, 0)),
        out_shape=jax.ShapeDtypeStruct((n, d), F32),
        scratch_shapes=[pltpu.VMEM((FFN_TM, D_FF), BF16)],
        compiler_params=_params("parallel"),
        name="ffn",
    )(h, g.reshape(1, d), w_in, w_in, w_out, final_g.reshape(1, d))


def _inproj_kernel(h_ref, g_ref, w_ref, za_ref, zb_ref, gp_ref):
    u = _rms(h_ref[...], g_ref[...]).astype(BF16)
    c0, c1 = A_SHIFT_WIDTH, A_SHIFT_WIDTH + B_IN_WIDTH
    za_ref[...] = jnp.dot(u, w_ref[:, :c0], preferred_element_type=F32)
    zb_ref[...] = jnp.dot(u, w_ref[:, c0:c1], preferred_element_type=F32).astype(zb_ref.dtype)
    gp_ref[...] = jnp.dot(u, w_ref[:, c1:], preferred_element_type=F32).astype(gp_ref.dtype)


def _inproj(h, g, w):
    n, d = h.shape
    widths = (A_SHIFT_WIDTH, B_IN_WIDTH, w.shape[1] - A_SHIFT_WIDTH - B_IN_WIDTH)
    return pl.pallas_call(
        _inproj_kernel,
        grid=(n // INPROJ_TM,),
        in_specs=[
            pl.BlockSpec((INPROJ_TM, d), lambda i: (i, 0)),
            _const_spec((1, d)),
            _const_spec(w.shape),
        ],
        out_specs=[pl.BlockSpec((INPROJ_TM, wd), lambda i: (i, 0)) for wd in widths],
        out_shape=[jax.ShapeDtypeStruct((n, wd), dt) for wd, dt in zip(widths, (F32, BF16, BF16))],
        compiler_params=_params("parallel"),
        name="inproj",
    )(h, g.reshape(1, d), w)


def _seg_sum(x, seg_ref):
    wd = seg_ref.shape[0]
    return jnp.concatenate([_mm(x[:, i:i + wd], seg_ref[...]) for i in range(0, x.shape[1], wd)], axis=1)


def _seg_sum_two(x1, x2, seg2_ref):
    wd = seg2_ref.shape[0] // 2
    o1, o2 = [], []
    for i in range(0, x1.shape[1], wd):
        s = _mm(jnp.concatenate([x1[:, i:i + wd], x2[:, i:i + wd]], axis=1), seg2_ref[...])
        o1.append(s[:, :wd])
        o2.append(s[:, wd:])
    return jnp.concatenate(o1, axis=1), jnp.concatenate(o2, axis=1)


def _rwkv_kernel(za_ref, mu_ref, w0_ref, wup_ref, a0_ref, aup_ref, gup_ref, kk_ref, ka_ref, rk_ref, ln_ref,
                 seg_ref, seg2_ref, tri_ref, y_ref, carry_ref, state_ref):
    L = RWKV_CHUNK
    hd = A_HEAD_DIM
    W2 = 2 * hd
    assert W2 == LANES and L == hd

    @pl.when(pl.program_id(1) == 0)
    def _():
        carry_ref[...] = jnp.zeros_like(carry_ref)
        state_ref[...] = jnp.zeros_like(state_ref)

    z = za_ref[...]
    tb = z.shape[0]
    row = lax.broadcasted_iota(jnp.int32, (tb, 1), 0)
    prev = jnp.where(row == 0, carry_ref[...], pltpu.roll(z, 1, 0))
    carry_ref[...] = z[tb - 1:tb, :]
    xs = z + (prev - z) * mu_ref[...]

    r = xs[:, 0:A_WIDTH]
    k = xs[:, A_WIDTH:2 * A_WIDTH]
    v = xs[:, 2 * A_WIDTH:3 * A_WIDTH]
    o = 3 * A_WIDTH
    w_lo = xs[:, o:o + A_RANK_W]
    a_lo = xs[:, o + A_RANK_W:o + A_RANK_W + A_RANK_A]
    g_lo = xs[:, o + A_RANK_W + A_RANK_A:]

    w = w0_ref[...] + _mm(jnp.tanh(w_lo), wup_ref[...])
    logd = -_sigmoid(w) * math.exp(-0.5)
    a = _sigmoid(a0_ref[...] + _mm(a_lo, aup_ref[...]))
    g = _mm(_sigmoid(g_lo), gup_ref[...])

    kk = k * kk_ref[...]
    kk = kk * lax.rsqrt(jnp.maximum(_seg_sum(kk * kk, seg_ref), 1e-24))
    k = k * (1.0 + (a - 1.0) * ka_ref[...])
    kka = kk * a

    tr = tri_ref.shape[0]
    cum = jnp.concatenate([_mm(tri_ref[...], logd[i:i + tr], pb=2) for i in range(0, tb, tr)], axis=0)
    tot = jnp.concatenate([jnp.broadcast_to(cum[c + L - 1:c + L, :], (L, A_WIDTH)) for c in range(0, tb, L)], axis=0)
    e_inc = jnp.exp(cum)
    e_exc = jnp.exp(cum - logd)
    e_neg = jnp.exp(-cum)
    e_end = jnp.exp(tot - cum)
    p_end = jnp.exp(tot)
    rt = r * e_inc
    at = -kk * e_exc
    bt = kka * e_neg
    kt = k * e_neg
    bh = kka * e_end
    kh = k * e_end

    def iota(shape, dim):
        return lax.broadcasted_iota(jnp.int32, shape, dim)

    first1 = (iota((L, W2), 1) & (W2 - 1)) < hd
    first2 = (iota((L, 2 * W2), 1) & (W2 - 1)) < hd
    strict_lower = (iota((L, 2 * W2), 1) & (L - 1)) < iota((L, 2 * W2), 0)
    lower = (iota((L, 2 * W2), 1) & (L - 1)) <= iota((L, 2 * W2), 0)
    eye_pair = jnp.where((iota((L, W2), 1) & (L - 1)) == iota((L, W2), 0), 1.0, 0.0)
    eye_full = iota((W2, W2), 0) == iota((W2, W2), 1)
    same_head = (iota((W2, 2 * W2), 0) < hd) == ((iota((W2, 2 * W2), 1) & (W2 - 1)) < hd)
    zeros_l = jnp.zeros((L, W2), BF16)
    zeros_2l = jnp.zeros((2 * L, W2), BF16)

    def stack(x):
        first = first1 if x.shape[1] == W2 else first2
        return jnp.concatenate([jnp.where(first, x, 0.0), jnp.where(first, 0.0, x)], axis=0)

    units = [(c, p) for c in range(tb // L) for p in range(A_WIDTH // W2)]

    def sl(x, u):
        c, p = u
        return x[c * L:(c + 1) * L, p * W2:(p + 1) * W2]

    at_b, rt_b, bt_b, kt_b, bh_b, kh_b, v_b = (x.astype(BF16) for x in (at, rt, bt, kt, bh, kh, v))

    a_ab, a_ak, a_r = {}, {}, {}
    for u in units:
        lhs = jnp.concatenate([sl(at_b, u), sl(rt_b, u)], axis=0)
        rhs = jnp.concatenate([stack(sl(bt_b, u)), stack(sl(kt_b, u))], axis=0)
        m1 = _mm(lhs, rhs, _NT)
        a_a = jnp.where(strict_lower, m1[:L], 0.0)
        a_ab[u] = a_a[:, :W2]
        a_ak[u] = a_a[:, W2:].astype(BF16)
        a_r[u] = jnp.where(lower, m1[L:], 0.0).astype(BF16)
    x1 = {u: _mm(a_ak[u], stack(sl(v_b, u))) for u in units}

    tinv = {u: eye_pair + a_ab[u] for u in units}
    pw = {u: a_ab[u].astype(BF16) for u in units}
    pw = {u: _mm(pw[u], stack(pw[u])).astype(BF16) for u in units}
    for lvl in range(1, RWKV_SOLVE_LEVELS):
        for u in units:
            t_b = tinv[u].astype(BF16)
            if lvl + 1 < RWKV_SOLVE_LEVELS:
                prod = _mm(pw[u], stack(jnp.concatenate([t_b, pw[u]], axis=1)))
                tinv[u] = tinv[u] + prod[:, :W2]
                pw[u] = prod[:, W2:].astype(BF16)
            else:
                tinv[u] = tinv[u] + _mm(pw[u], stack(t_b))
    sol = {u: _mm(tinv[u], stack(jnp.concatenate([sl(at_b, u), x1[u].astype(BF16)], axis=1))).astype(BF16)
           for u in units}

    qp, yp, gbd, hp = {}, {}, {}, {}
    for u in units:
        v_u = sl(v_b, u)
        rhs_a = jnp.concatenate([stack(sol[u]), jnp.concatenate([zeros_2l, stack(v_u)], axis=1)], axis=0)
        m2a = _mm(a_r[u], rhs_a)
        rhs_b = jnp.concatenate([sol[u], jnp.concatenate([zeros_l, v_u], axis=1)], axis=0)
        m2b = _mm(jnp.concatenate([sl(bh_b, u), sl(kh_b, u)], axis=0), rhs_b, _TN)
        m2b = jnp.where(same_head, m2b, 0.0)
        qp[u] = sl(rt, u) + m2a[:, :W2]
        yp[u] = m2a[:, W2:]
        c, p = u
        gbd[u] = jnp.where(eye_full, p_end[c * L:c * L + 1, p * W2:(p + 1) * W2], 0.0) + m2b[:, :W2]
        hp[u] = m2b[:, W2:]

    ys = {}
    for u in units:
        m3 = _mm(jnp.concatenate([qp[u], gbd[u]], axis=0), state_ref[u[1]])
        ys[u] = m3[:L] + yp[u]
        state_ref[u[1]] = m3[L:] + hp[u]

    y = jnp.concatenate([jnp.concatenate([ys[(c, p)] for p in range(A_WIDTH // W2)], axis=1)
                         for c in range(tb // L)], axis=0)
    inv_n = 1.0 / hd
    y_sum, rk_sum = _seg_sum_two(y, r * k * rk_ref[...], seg2_ref)
    yc = y - y_sum * inv_n
    var = _seg_sum(yc * yc, seg_ref) * inv_n
    yn = yc * lax.rsqrt(var + A_LN_EPS) * ln_ref[0:1, :] + ln_ref[1:2, :]
    y_ref[...] = ((yn + rk_sum * v) * g).astype(y_ref.dtype)


def _rwkv(za, batch, mu, w0, w_up, a0, a_up, g_up, k_k, k_a, r_k, ln, seg, seg2, tri):
    n = za.shape[0]
    nt = n // batch // SEQ_TB
    row = lambda p: p.reshape(1, -1)
    consts = [row(mu), row(w0), w_up, row(a0), a_up, g_up, row(k_k), row(k_a), row(r_k), ln, seg, seg2, tri]
    return pl.pallas_call(
        _rwkv_kernel,
        grid=(batch, nt),
        in_specs=[pl.BlockSpec((SEQ_TB, A_SHIFT_WIDTH), lambda b, t: (b * nt + t, 0))]
        + [_const_spec(c.shape) for c in consts],
        out_specs=pl.BlockSpec((SEQ_TB, A_WIDTH), lambda b, t: (b * nt + t, 0)),
        out_shape=jax.ShapeDtypeStruct((n, A_WIDTH), BF16),
        scratch_shapes=[pltpu.VMEM((1, A_SHIFT_WIDTH), F32),
                        pltpu.VMEM((A_HEADS // 2, 2 * A_HEAD_DIM, 2 * A_HEAD_DIM), F32)],
        compiler_params=_params("parallel", "arbitrary"),
        name="rwkv",
    )(za, *consts)


def _rope_kernel(pos_ref, invf_ref, sign_ref, cos_ref, sin_ref):
    ang = pos_ref[...].astype(F32) * invf_ref[...]
    cos_ref[...] = jnp.cos(ang)
    sin_ref[...] = jnp.sin(ang) * sign_ref[...]


def _rope_tables(pos, invf, sign):
    n = pos.shape[0]
    return pl.pallas_call(
        _rope_kernel,
        grid=(n // ROPE_TM,),
        in_specs=[pl.BlockSpec((ROPE_TM, 1), lambda i: (i, 0)), _const_spec((1, LANES)), _const_spec((1, LANES))],
        out_specs=[pl.BlockSpec((ROPE_TM, LANES), lambda i: (i, 0))] * 2,
        out_shape=[jax.ShapeDtypeStruct((n, LANES), F32)] * 2,
        compiler_params=_params("parallel"),
        name="rope",
    )(pos, invf, sign)


def _ret_kernel(zb_ref, cos_ref, sin_ref, dmask_ref, qdec_ref, kdec_ref, cdec_ref, gn_ref, segm_ref, y_ref, state_ref):
    C = B_CHUNK
    dk, dv = B_QK_DIM, B_V_DIM
    QW, VW = 2 * dk, 2 * dv
    assert QW == LANES and dv == LANES

    @pl.when(pl.program_id(1) == 0)
    def _():
        state_ref[...] = jnp.zeros_like(state_ref)

    def iota(shape, dim):
        return lax.broadcasted_iota(jnp.int32, shape, dim)

    first_half = (iota((C, B_QK_WIDTH), 1) & (dk - 1)) < dk // 2
    first_q = iota((C, QW), 1) < dk
    first_v = iota((C, VW), 1) < dv
    same_head = (iota((QW, VW), 0) < dk) == (iota((QW, VW), 1) < dv)

    def rotate(x, cos, sin):
        partner = jnp.where(first_half, pltpu.roll(x, B_QK_WIDTH - dk // 2, 1), pltpu.roll(x, dk // 2, 1))
        return x * cos + partner * sin

    def stack(x, first):
        return jnp.concatenate([jnp.where(first, x, 0.0), jnp.where(first, 0.0, x)], axis=0)

    chunks = range(zb_ref.shape[0] // C)
    q, k, qd, kd = {}, {}, {}, {}
    for c in chunks:
        rs = slice(c * C, (c + 1) * C)
        reps = B_QK_WIDTH // LANES
        cos = jnp.concatenate([cos_ref[rs, :]] * reps, axis=1)
        sin = jnp.concatenate([sin_ref[rs, :]] * reps, axis=1)
        q[c] = rotate(zb_ref[rs, 0:B_QK_WIDTH].astype(F32), cos, sin)
        k[c] = rotate(zb_ref[rs, B_QK_WIDTH:2 * B_QK_WIDTH].astype(F32), cos, sin) * (dk ** -0.5)
        qd[c] = q[c] * qdec_ref[...]
        kd[c] = k[c] * kdec_ref[...]

    def qk(x, u):
        return x[u[0]][:, u[1] * QW:(u[1] + 1) * QW]

    def vg(base, u):
        c, p = u
        return zb_ref[c * C:(c + 1) * C, base + p * VW:base + (p + 1) * VW]

    for g0 in range(0, len(chunks), RET_GROUP):
        group = range(g0, g0 + RET_GROUP)
        units = [(c, p) for c in group for p in range(B_HEADS // 2)]
        v = {u: vg(2 * B_QK_WIDTH, u) for u in units}
        scores = {u: _mm(qk(q, u), stack(qk(k, u), first_q), _NT) * dmask_ref[u[1]] for u in units}
        inner = {u: _mm(scores[u], stack(v[u], first_v)) for u in units}
        kv = {u: jnp.where(same_head, _mm(qk(kd, u), v[u], _TN), 0.0) for u in units}

        before = {}
        for p in range(B_HEADS // 2):
            st = state_ref[p]
            for c in group:
                before[(c, p)] = st
                st = st * cdec_ref[p] + kv[(c, p)]
            state_ref[p] = st

        y = {u: inner[u] + _mm(qk(qd, u), before[u]) for u in units}
        yc = {u: y[u] - _mm(y[u], segm_ref[...]) for u in units}
        var = {u: _mm(yc[u] * yc[u], segm_ref[...]) for u in units}
        for u in units:
            c, p = u
            g = vg(2 * B_QK_WIDTH + B_V_WIDTH, u).astype(F32)
            out = yc[u] * lax.rsqrt(var[u] + B_GN_EPS) * gn_ref[:, p * VW:(p + 1) * VW] * (g * _sigmoid(g))
            y_ref[c * C:(c + 1) * C, p * VW:(p + 1) * VW] = out.astype(y_ref.dtype)


def _retention_tables():
    heads = np.arange(B_HEADS, dtype=np.float64)
    log_gamma = np.log1p(-np.exp2(-5.0 - heads))
    pos = np.arange(B_CHUNK, dtype=np.float64)
    diff = pos[:, None] - pos[None, :]
    dmask = np.where(diff[None] >= 0, np.exp(np.maximum(diff, 0.0)[None] * log_gamma[:, None, None]), 0.0)
    dmask = dmask.reshape(B_HEADS // 2, 2, B_CHUNK, B_CHUNK).transpose(0, 2, 1, 3).reshape(B_HEADS // 2, B_CHUNK, -1)
    q_decay = np.exp((pos + 1.0)[:, None] * log_gamma[None, :])
    k_decay = np.exp((B_CHUNK - 1.0 - pos)[:, None] * log_gamma[None, :])
    chunk_decay = np.exp(B_CHUNK * log_gamma).reshape(B_HEADS // 2, 2, 1, 1)
    cdec = np.broadcast_to(chunk_decay, (B_HEADS // 2, 2, B_QK_DIM, 2 * B_V_DIM)).reshape(B_HEADS // 2, 2 * B_QK_DIM, -1)
    segm = np.kron(np.eye(2), np.full((B_V_DIM, B_V_DIM), 1.0 / B_V_DIM))
    rep = lambda t: np.repeat(t, B_QK_DIM, axis=1)
    return (jnp.asarray(dmask, F32), jnp.asarray(rep(q_decay), F32), jnp.asarray(rep(k_decay), F32),
            jnp.asarray(cdec, F32), jnp.asarray(segm, BF16))


def _retention(zb, batch, cos, sin, gn):
    n = zb.shape[0]
    nt = n // batch // SEQ_TB
    dmask, qdec, kdec, cdec, segm = _retention_tables()
    consts = [dmask, qdec, kdec, cdec, gn.reshape(1, -1), segm]
    tok = lambda wd: pl.BlockSpec((SEQ_TB, wd), lambda b, t: (b * nt + t, 0))
    return pl.pallas_call(
        _ret_kernel,
        grid=(batch, nt),
        in_specs=[tok(B_IN_WIDTH), tok(LANES), tok(LANES)] + [_const_spec(c.shape) for c in consts],
        out_specs=tok(B_V_WIDTH),
        out_shape=jax.ShapeDtypeStruct((n, B_V_WIDTH), BF16),
        scratch_shapes=[pltpu.VMEM((B_HEADS // 2, 2 * B_QK_DIM, 2 * B_V_DIM), F32)],
        compiler_params=_params("parallel", "arbitrary"),
        name="retention",
    )(zb, cos, sin, *consts)


def _merge_kernel(h_ref, ya_ref, yb_ref, gp_ref, gb_ref, pa_ref, pb_ref, wo_ref, o_ref):
    d = h_ref.shape[1]
    gates = _sigmoid(gp_ref[...] + gb_ref[...])
    merged = (gates[:, :d] * _mm(ya_ref[...], pa_ref[...]) + gates[:, d:] * _mm(yb_ref[...], pb_ref[...]))
    o_ref[...] = h_ref[...] + _mm(merged, wo_ref[...])


def _merge(h, ya, yb, gp, gate_b, p_a, p_b, w_o):
    n, d = h.shape
    tok = lambda wd: pl.BlockSpec((MERGE_TM, wd), lambda i: (i, 0))
    return pl.pallas_call(
        _merge_kernel,
        grid=(n // MERGE_TM,),
        in_specs=[tok(d), tok(A_WIDTH), tok(B_V_WIDTH), tok(2 * d), _const_spec((1, 2 * d)),
                  _const_spec(p_a.shape), _const_spec(p_b.shape), _const_spec(w_o.shape)],
        out_specs=tok(d),
        out_shape=jax.ShapeDtypeStruct((n, d), F32),
        compiler_params=_params("parallel"),
        name="merge",
    )(h, ya, yb, gp, gate_b.reshape(1, 2 * d), p_a, p_b, w_o)


def _memkv_kernel(m_ref, g_ref, w_ref, o_ref):
    o_ref[...] = _mm(_rms(m_ref[...], g_ref[...]), w_ref[...]).astype(BF16)


def _memkv(mem, g, w_kv):
    n, d = mem.shape
    tm = 512
    return pl.pallas_call(
        _memkv_kernel,
        grid=(n // tm,),
        in_specs=[pl.BlockSpec((tm, d), lambda i: (i, 0)), _const_spec((1, d)), _const_spec(w_kv.shape)],
        out_specs=pl.BlockSpec((tm, 2 * d), lambda i: (i, 0)),
        out_shape=jax.ShapeDtypeStruct((n, 2 * d), BF16),
        compiler_params=_params("parallel"),
        name="memkv",
    )(mem, g.reshape(1, d), w_kv)


def _cross_kernel(h_ref, g_ref, wq_ref, kv_ref, wo_ref, o_ref):
    h = h_ref[...]
    d = h.shape[1]
    hd = d // X_HEADS
    q = _mm(_rms(h, g_ref[...]), wq_ref[...])
    heads = range(X_HEADS)
    s = [_mm(q[:, i * hd:(i + 1) * hd], kv_ref[:, i * hd:(i + 1) * hd], _NT) * (hd ** -0.5) for i in heads]
    e = [jnp.exp(s[i] - jnp.max(s[i], axis=-1, keepdims=True)) for i in heads]
    p = [e[i] * (1.0 / jnp.sum(e[i], axis=-1, keepdims=True)) for i in heads]
    outs = [_mm(p[i], kv_ref[:, d + i * hd:d + (i + 1) * hd]) for i in heads]
    o_ref[...] = h + _mm(jnp.concatenate(outs, axis=1), wo_ref[...])


def _cross(h, batch, g, w_q, kv, w_o):
    n, d = h.shape
    nt = n // batch // CROSS_TM
    mlen = kv.shape[0] // batch
    tok = pl.BlockSpec((CROSS_TM, d), lambda b, t: (b * nt + t, 0))
    return pl.pallas_call(
        _cross_kernel,
        grid=(batch, nt),
        in_specs=[tok, _const_spec((1, d)), _const_spec(w_q.shape),
                  pl.BlockSpec((mlen, 2 * d), lambda b, t: (b, 0)), _const_spec(w_o.shape)],
        out_specs=tok,
        out_shape=jax.ShapeDtypeStruct((n, d), F32),
        compiler_params=_params("parallel", "arbitrary"),
        name="cross",
    )(h, g.reshape(1, d), w_q, kv, w_o)


def kernel(x, mem, positions, norm_g, ffn_w_in, ffn_w_out, mix_w_in, mix_gate_b, shift_mu, a_w0, a_w_up, a_a0, a_a_up, a_g_up, a_k_k, a_k_a, a_r_k, a_ln, b_gn, w_branch_a, w_branch_b, mix_w_out, mem_norm, cross_w_q, cross_w_kv, cross_w_o, final_norm):
    batch, seq, d = x.shape
    depth = norm_g.shape[0]
    n = batch * seq
    assert seq % SEQ_TB == 0 and seq % CROSS_TM == 0 and n % ROPE_TM == 0
    assert n % FFN_TM == 0 and n % INPROJ_TM == 0 and n % MERGE_TM == 0
    assert SEQ_TB % B_CHUNK == 0 and SEQ_TB % TRI_ROWS == 0 and TRI_ROWS % RWKV_CHUNK == 0
    assert 2 ** RWKV_SOLVE_LEVELS == RWKV_CHUNK
    bf = lambda t: t.astype(BF16)

    half = B_QK_DIM // 2
    inv_freq = ROPE_BASE ** (-jnp.arange(half, dtype=F32) / half)
    invf = jnp.tile(inv_freq, LANES // half).reshape(1, LANES)
    lane = np.arange(LANES)
    sign = jnp.asarray(np.where((lane % B_QK_DIM) < half, -1.0, 1.0), F32).reshape(1, LANES)
    cos, sin = _rope_tables(positions.reshape(n, 1), invf, sign)

    seg = jnp.asarray(np.kron(np.eye(LANES // A_HEAD_DIM), np.ones((A_HEAD_DIM, A_HEAD_DIM))), BF16)
    seg2 = jnp.asarray(np.kron(np.eye(2 * LANES // A_HEAD_DIM), np.ones((A_HEAD_DIM, A_HEAD_DIM))), BF16)
    tri = jnp.asarray(np.kron(np.eye(TRI_ROWS // RWKV_CHUNK), np.tril(np.ones((RWKV_CHUNK, RWKV_CHUNK)))), BF16)
    mem2 = mem.reshape(-1, d)

    h = x.reshape(n, d)
    for l in range(depth):
        h = _ffn(h, norm_g[l, 0], bf(ffn_w_in[l, 0]), bf(ffn_w_out[l, 0]), final_norm, False)
        za, zb, gp = _inproj(h, norm_g[l, 1], bf(mix_w_in[l]))
        ya = _rwkv(za, batch, shift_mu[l], a_w0[l], bf(a_w_up[l]), a_a0[l], bf(a_a_up[l]), bf(a_g_up[l]),
                   a_k_k[l], a_k_a[l], a_r_k[l], a_ln[l], seg, seg2, tri)
        yb = _retention(zb, batch, cos, sin, b_gn[l])
        h = _merge(h, ya, yb, gp, mix_gate_b[l], bf(w_branch_a[l]), bf(w_branch_b[l]), bf(mix_w_out[l]))
        kv = _memkv(mem2, mem_norm[l], bf(cross_w_kv[l]))
        h = _cross(h, batch, norm_g[l, 2], bf(cross_w_q[l]), kv, bf(cross_w_o[l]))
        h = _ffn(h, norm_g[l, 3], bf(ffn_w_in[l, 1]), bf(ffn_w_out[l, 1]), final_norm, l == depth - 1)
    return h.reshape(batch, seq, d)
```
